```python
import math
import jax, jax.numpy as jnp
from jax import lax
import numpy as np

D_MODEL = 1024
BATCH = 16
SEQ = 4096
DEPTH = 2

CTX_LEN = 256
GRID_W = 64
EPS = 1e-6

A_WIDTH = D_MODEL // 4
A_HEADS = 4
A_HEAD_DIM = A_WIDTH // A_HEADS
CHUNK = 128

B_WIDTH = D_MODEL // 2
B_HEADS = 4
B_V_DIM = B_WIDTH // B_HEADS
B_QK_DIM = B_V_DIM // 2
ROPE_AXIS_DIM = B_QK_DIM // 2
ROPE_THETA = 10000.0
Q_BLOCK = 128

C_WIDTH = D_MODEL // 4
C_CONV = 31

MIX_WIDTH = A_WIDTH + B_WIDTH + C_WIDTH
QK_COLS = B_HEADS * 2 * B_QK_DIM
COL_A = 0
COL_Q = COL_A + 2 * A_WIDTH
COL_K = COL_Q + QK_COLS
COL_V = COL_K + QK_COLS
COL_C = COL_V + B_WIDTH
IN_COLS = COL_C + 2 * C_WIDTH

D_FF = ((8 * D_MODEL // 3 + 127) // 128) * 128
FFN_CONV = 3

kernel_name = "hybrid_diff_gmlp_conformer_dit"

F32 = jnp.float32


def rms_norm(x, g):
    xf = x.astype(F32)
    y = xf * lax.rsqrt(jnp.mean(xf * xf, axis=-1, keepdims=True) + EPS)
    return (y * g.astype(F32)).astype(x.dtype)


def layer_norm(x, g, b):
    xf = x.astype(F32)
    xc = xf - jnp.mean(xf, axis=-1, keepdims=True)
    y = xc * lax.rsqrt(jnp.mean(xc * xc, axis=-1, keepdims=True) + EPS)
    return (y * g.astype(F32) + b.astype(F32)).astype(x.dtype)


def dwconv(x, w, b):
    k = w.shape[0]
    y = lax.conv_general_dilated(
        x, w[:, None, :].astype(x.dtype), window_strides=(1,), padding=[(k // 2, k // 2)],
        dimension_numbers=("NWC", "WIO", "NWC"), feature_group_count=x.shape[-1])
    return y + b


def adaln(cvec, w, b, n):
    m = jax.nn.silu(cvec) @ w[:, : n * D_MODEL] + b[: n * D_MODEL]
    return jnp.split(m, n, axis=-1)


def modulate(x, shift, scale):
    return x * (1 + scale) + shift


def axial_rope_tables(n_tokens, dtype):
    rows = n_tokens // GRID_W
    row = jnp.repeat(jnp.arange(rows), GRID_W).astype(F32)
    col = jnp.tile(jnp.arange(GRID_W), rows).astype(F32)
    inv = ROPE_THETA ** (-jnp.arange(0, ROPE_AXIS_DIM, 2, dtype=F32) / ROPE_AXIS_DIM)
    ang = jnp.stack([row[:, None] * inv, col[:, None] * inv], axis=1)
    ang = ang[None, :, None, None]
    return (jnp.cos(ang).astype(dtype), jnp.sin(ang).astype(dtype))


def axial_rope(t, cos, sin):
    sh = t.shape
    t = t.reshape(*sh[:-1], 2, 2, ROPE_AXIS_DIM // 2)
    t1, t2 = t[..., 0, :], t[..., 1, :]
    out = jnp.stack([t1 * cos - t2 * sin, t2 * cos + t1 * sin], axis=-2)
    return out.reshape(sh)


def qk_heads(p, g, rope):
    b, n, _ = p.shape
    t = rms_norm(p.reshape(b, n, B_HEADS, 2, B_QK_DIM), g)
    return t if rope is None else axial_rope(t, *rope)


def project(h, P, rope):
    p = h @ P["w_in"]
    b, n, _ = p.shape
    q = qk_heads(p[..., COL_Q:COL_K], P["q_norm_g"], rope)
    k = qk_heads(p[..., COL_K:COL_V], P["k_norm_g"], rope)
    v = p[..., COL_V:COL_C].reshape(b, n, B_HEADS, B_V_DIM)
    return (p[..., COL_A:COL_Q], q, k, v, p[..., COL_C:])


def diff_attn(q, k, v, lam):
    s = jnp.einsum("bqhmd,bkhmd->bhmqk", q.astype(F32), k.astype(F32)) * (B_QK_DIM ** -0.5)
    p = jax.nn.softmax(s, axis=-1)
    a = p[:, :, 0] - lam * p[:, :, 1]
    return jnp.einsum("bhqk,bkhd->bqhd", a, v.astype(F32))


def blocked_diff_attn(q, k, v, lam):
    b, n = q.shape[:2]
    qb = jnp.moveaxis(q.reshape(b, n // Q_BLOCK, Q_BLOCK, *q.shape[2:]), 1, 0)
    o = lax.map(lambda qi: diff_attn(qi, k, v, lam), qb)
    return jnp.moveaxis(o, 0, 1).reshape(b, n, B_HEADS, B_V_DIM)


def chunk_gmlp(pa, P):
    z = jax.nn.gelu(pa)
    u, v = jnp.split(z, 2, axis=-1)
    v = layer_norm(v, P["ln_v_g"], P["ln_v_b"])
    b, n, _ = v.shape
    vb = v.reshape(b, n // CHUNK, CHUNK, A_HEADS, A_HEAD_DIM)
    gate = jnp.einsum("hpq,bnqhd->bnphd", P["w_s"], vb) + P["b_s"].T[:, :, None]
    return u * gate.reshape(b, n, A_WIDTH)


def conformer_conv(pc, P):
    a, g = jnp.split(pc, 2, axis=-1)
    y = dwconv(a * jax.nn.sigmoid(g), P["conv_w"], P["conv_b"])
    return jax.nn.silu(layer_norm(y, P["ln_c_g"], P["ln_c_b"]))


def mixer_merge(pa, o, pc, P, lam_init):
    b, n, _ = pa.shape
    ya = chunk_gmlp(pa, P)
    yb = (rms_norm(o, P["subln_g"]) * (1.0 - lam_init)).astype(pa.dtype).reshape(b, n, B_WIDTH)
    yc = conformer_conv(pc, P)
    return jnp.concatenate([ya, yb, yc], axis=-1) @ P["w_out"]


def conv_ffn(h, P):
    gt = dwconv(h @ P["w_gate"], P["ffn_conv_w"], P["ffn_conv_b"])
    return (jax.nn.silu(gt) * (h @ P["w_val"])) @ P["w_down"]


def setup_inputs(seed: int = 0) -> dict:
    key = jax.random.key(seed)
    ks = jax.random.split(key, 30)
    L = DEPTH

    def nrm(i, shape, s):
        return jax.random.normal(ks[i], shape, F32) * s

    return {
        "x": nrm(0, (BATCH, SEQ, D_MODEL), 1.0),
        "c": nrm(1, (BATCH, D_MODEL), 1.0),
        "ctx": nrm(2, (BATCH, CTX_LEN, D_MODEL), 1.0),
        "c_ctx": nrm(3, (D_MODEL,), 1.0),
        "w_mod": nrm(4, (L, D_MODEL, 6 * D_MODEL), D_MODEL ** -0.5),
        "b_mod": nrm(5, (L, 6 * D_MODEL), 0.02),
        "norm1_g": 1.0 + nrm(6, (L, D_MODEL), 0.05),
        "w_in": nrm(7, (L, D_MODEL, IN_COLS), D_MODEL ** -0.5),
        "ln_v_g": 1.0 + nrm(8, (L, A_WIDTH), 0.05),
        "ln_v_b": nrm(9, (L, A_WIDTH), 0.02),
        "w_s": nrm(10, (L, A_HEADS, CHUNK, CHUNK), CHUNK ** -0.5),
        "b_s": nrm(11, (L, A_HEADS, CHUNK), 0.02),
        "q_norm_g": 1.0 + nrm(12, (L, B_QK_DIM), 0.05),
        "k_norm_g": 1.0 + nrm(13, (L, B_QK_DIM), 0.05),
        "lam_q1": nrm(14, (L, B_QK_DIM), 0.1),
        "lam_k1": nrm(15, (L, B_QK_DIM), 0.1),
        "lam_q2": nrm(16, (L, B_QK_DIM), 0.1),
        "lam_k2": nrm(17, (L, B_QK_DIM), 0.1),
        "subln_g": 1.0 + nrm(18, (L, B_V_DIM), 0.05),
        "conv_w": nrm(19, (L, C_CONV, C_WIDTH), C_CONV ** -0.5),
        "conv_b": nrm(20, (L, C_WIDTH), 0.02),
        "ln_c_g": 1.0 + nrm(21, (L, C_WIDTH), 0.05),
        "ln_c_b": nrm(22, (L, C_WIDTH), 0.02),
        "w_out": nrm(23, (L, MIX_WIDTH, D_MODEL), MIX_WIDTH ** -0.5),
        "norm2_g": 1.0 + nrm(24, (L, D_MODEL), 0.05),
        "w_gate": nrm(25, (L, D_MODEL, D_FF), D_MODEL ** -0.5),
        "w_val": nrm(26, (L, D_MODEL, D_FF), D_MODEL ** -0.5),
        "ffn_conv_w": nrm(27, (L, FFN_CONV, D_FF), FFN_CONV ** -0.5),
        "ffn_conv_b": nrm(28, (L, D_FF), 0.02),
        "w_down": nrm(29, (L, D_FF, D_MODEL), D_FF ** -0.5),
    }


def reference(x, c, ctx, c_ctx, w_mod, b_mod, norm1_g, w_in, ln_v_g, ln_v_b, w_s, b_s,
              q_norm_g, k_norm_g, lam_q1, lam_k1, lam_q2, lam_k2, subln_g, conv_w, conv_b,
              ln_c_g, ln_c_b, w_out, norm2_g, w_gate, w_val, ffn_conv_w, ffn_conv_b, w_down):
    rope = axial_rope_tables(x.shape[1], x.dtype)
    x_lat, x_ctx = x, ctx
    for l in range(DEPTH):
        last = l == DEPTH - 1
        P = {
            "w_in": w_in[l], "ln_v_g": ln_v_g[l], "ln_v_b": ln_v_b[l], "w_s": w_s[l], "b_s": b_s[l],
            "q_norm_g": q_norm_g[l], "k_norm_g": k_norm_g[l], "subln_g": subln_g[l],
            "conv_w": conv_w[l], "conv_b": conv_b[l], "ln_c_g": ln_c_g[l], "ln_c_b": ln_c_b[l],
            "w_out": w_out[l], "w_gate": w_gate[l], "w_val": w_val[l],
            "ffn_conv_w": ffn_conv_w[l], "ffn_conv_b": ffn_conv_b[l], "w_down": w_down[l],
        }
        lam_init = 0.8 - 0.6 * math.exp(-0.3 * l)
        lam = (jnp.exp(jnp.sum(lam_q1[l].astype(F32) * lam_k1[l].astype(F32)))
               - jnp.exp(jnp.sum(lam_q2[l].astype(F32) * lam_k2[l].astype(F32))) + lam_init)

        cmod = adaln(c_ctx, w_mod[l], b_mod[l], 2 if last else 6)
        hc = modulate(rms_norm(x_ctx, norm1_g[l]), cmod[0], cmod[1])
        if last:
            pkv = hc @ w_in[l][:, COL_K:COL_C]
            k_c = qk_heads(pkv[..., :QK_COLS], k_norm_g[l], None)
            v_c = pkv[..., QK_COLS:].reshape(hc.shape[0], hc.shape[1], B_HEADS, B_V_DIM)
        else:
            pa_c, q_c, k_c, v_c, pc_c = project(hc, P, None)
            o_c = diff_attn(q_c, k_c, v_c, lam)
            x_ctx = x_ctx + cmod[2] * mixer_merge(pa_c, o_c, pc_c, P, lam_init)
            h2c = modulate(rms_norm(x_ctx, norm2_g[l]), cmod[3], cmod[4])
            x_ctx = x_ctx + cmod[5] * conv_ffn(h2c, P)

        sh1, sc1, g1, sh2, sc2, g2 = [m[:, None, :] for m in adaln(c, w_mod[l], b_mod[l], 6)]
        hl = modulate(rms_norm(x_lat, norm1_g[l]), sh1, sc1)
        pa, q, k, v, pc = project(hl, P, rope)
        k_all = jnp.concatenate([k, k_c], axis=1)
        v_all = jnp.concatenate([v, v_c], axis=1)
        o = blocked_diff_attn(q, k_all, v_all, lam)
        x_lat = x_lat + g1 * mixer_merge(pa, o, pc, P, lam_init)
        h2 = modulate(rms_norm(x_lat, norm2_g[l]), sh2, sc2)
        x_lat = x_lat + g2 * conv_ffn(h2, P)
    return x_lat
```

```python
import functools
import math

import jax
import jax.numpy as jnp
from jax import lax
from jax.experimental import pallas as pl
from jax.experimental.pallas import tpu as pltpu

F32 = jnp.float32
BF16 = jnp.bfloat16

EPS = 1e-6
GRID_W = 64
ROPE_THETA = 10000.0
A_HEADS = 4
CHUNK = 128
B_HEADS = 4
B_QK_DIM = 64
B_V_DIM = 128
C_CONV = 31
FFN_CONV = 3

V7X_MXU_DIM = 256
V7X_VMEM_LIMIT_BYTES = 56 * 1024 * 1024
BF16_SUBLANES = 16
HALO = 16
MOD_ROWS = 24


def _cparams(sem):
    return pltpu.CompilerParams(dimension_semantics=sem, vmem_limit_bytes=V7X_VMEM_LIMIT_BYTES)


def _rms(x, g):
    return x * lax.rsqrt(jnp.mean(x * x, axis=-1, keepdims=True) + EPS) * g


def _ln(x, g, b):
    xc = x - jnp.mean(x, axis=-1, keepdims=True)
    return xc * lax.rsqrt(jnp.mean(xc * xc, axis=-1, keepdims=True) + EPS) * g + b


def _silu(x):
    return x * jax.nn.sigmoid(x)


def _adaln_kernel(c_ref, w_ref, b_ref, q1_ref, k1_ref, q2_ref, k2_ref, o_ref, lam_ref):
    s = _silu(c_ref[...])
    o_ref[0] = jnp.dot(s.astype(BF16), w_ref[0].astype(BF16), preferred_element_type=F32) + b_ref[0]
    d1 = jnp.sum(q1_ref[0] * k1_ref[0], axis=-1, keepdims=True)
    d2 = jnp.sum(q2_ref[0] * k2_ref[0], axis=-1, keepdims=True)
    lam_ref[0] = jnp.broadcast_to(jnp.exp(d1) - jnp.exp(d2), lam_ref.shape[1:])


def _adaln(cvec, w_mod, b_mod, lq1, lk1, lq2, lk2):
    L, D, N6 = w_mod.shape
    tn = N6 // 4
    small = pl.BlockSpec((1, 1, B_QK_DIM), lambda l, j: (l, 0, 0))
    return pl.pallas_call(
        _adaln_kernel,
        grid=(L, N6 // tn),
        in_specs=[
            pl.BlockSpec((MOD_ROWS, D), lambda l, j: (0, 0)),
            pl.BlockSpec((1, D, tn), lambda l, j: (l, 0, j)),
            pl.BlockSpec((1, 1, tn), lambda l, j: (l, 0, j)),
            small, small, small, small,
        ],
        out_specs=[
            pl.BlockSpec((1, MOD_ROWS, tn), lambda l, j: (l, 0, j)),
            pl.BlockSpec((1, 8, 128), lambda l, j: (l, 0, 0)),
        ],
        out_shape=[
            jax.ShapeDtypeStruct((L, MOD_ROWS, N6), F32),
            jax.ShapeDtypeStruct((L, 8, 128), F32),
        ],
        compiler_params=_cparams(("arbitrary", "arbitrary")),
        name="adaln",
    )(cvec, w_mod, b_mod.reshape(L, 1, N6), lq1.reshape(L, 1, -1), lk1.reshape(L, 1, -1),
      lq2.reshape(L, 1, -1), lk2.reshape(L, 1, -1))


def _proj_kernel(*refs, rope, D):
    (x_ref, mod_ref, g1_ref, w_ref, lng_ref, lnb_ref, ws_ref, bs_ref, gq_ref, gk_ref, gsum_ref) = refs[:11]
    if rope:
        cos_ref, sa_ref, sb_ref = refs[11:14]
        outs = refs[14:]
    else:
        outs = refs[11:]
    ya_ref, q_ref, k_ref, v_ref, glu_ref = outs

    aw = D // 4
    bw = D // 2
    x = x_ref[0]
    tm = x.shape[0]
    mod = mod_ref[0]
    shift, scale = mod[:, 0:D], mod[:, D:2 * D]
    h = _rms(x, g1_ref[...]) * (1.0 + scale) + shift
    p = jnp.dot(h.astype(BF16), w_ref[...], preferred_element_type=F32)

    z = jax.nn.gelu(p[:, 0:2 * aw])
    u, v = z[:, :aw], z[:, aw:]
    vn = _ln(v, lng_ref[...], lnb_ref[...]).astype(BF16)
    head = lax.broadcasted_iota(jnp.int32, (CHUNK, aw), 1) // (aw // A_HEADS)
    ws = ws_ref[...]
    for c in range(tm // CHUNK):
        r = jnp.dot(ws, vn[c * CHUNK:(c + 1) * CHUNK, :], preferred_element_type=F32)
        gate = r[0:CHUNK]
        for hh in range(1, A_HEADS):
            gate = jnp.where(head == hh, r[hh * CHUNK:(hh + 1) * CHUNK], gate)
        gate = gate + bs_ref[...]
        ya_ref[0, c * CHUNK:(c + 1) * CHUNK, :] = (u[c * CHUNK:(c + 1) * CHUNK] * gate).astype(BF16)

    c_q = 2 * aw
    c_k = c_q + bw
    c_v = c_k + bw
    c_c = c_v + bw

    def qk_norm(t, g):
        sq = (t * t).astype(BF16)
        parts = [jnp.dot(sq[:, j:j + V7X_MXU_DIM], gsum_ref[...], preferred_element_type=F32)
                 for j in range(0, bw, V7X_MXU_DIM)]
        ss = jnp.concatenate(parts, axis=-1)
        t = t * lax.rsqrt(ss * (1.0 / B_QK_DIM) + EPS) * g
        if not rope:
            return t
        cos, sa, sb = cos_ref[...], sa_ref[...], sb_ref[...]
        blocks = []
        for j in range(0, bw, 128):
            tb = t[:, j:j + 128]
            blocks.append(tb * cos + pltpu.roll(tb, 128 - 16, 1) * sa + pltpu.roll(tb, 16, 1) * sb)
        return jnp.concatenate(blocks, axis=-1)

    q_ref[0] = qk_norm(p[:, c_q:c_k], gq_ref[...]).astype(BF16)
    k_ref[0] = qk_norm(p[:, c_k:c_v], gk_ref[...]).astype(BF16)
    v_ref[0] = p[:, c_v:c_c].astype(BF16)

    glu_ref[0] = (p[:, c_c:c_c + aw] * jax.nn.sigmoid(p[:, c_c + aw:c_c + 2 * aw])).astype(BF16)


def _proj(x, mod, g1, w_in, lng, lnb, ws_all, bs_full, gq, gk, gsum, rope_tabs, tm):
    nb, n, D = x.shape
    aw, bw = D // 4, D // 2
    cols = w_in.shape[1]
    rope = rope_tabs is not None
    const = lambda b, i: (0, 0)
    tok = lambda b, i: (b, i, 0)
    in_specs = [
        pl.BlockSpec((1, tm, D), tok),
        pl.BlockSpec((1, 1, 6 * D), lambda b, i: (b, 0, 0)),
        pl.BlockSpec((1, D), const),
        pl.BlockSpec((D, cols), const),
        pl.BlockSpec((1, aw), const),
        pl.BlockSpec((1, aw), const),
        pl.BlockSpec((A_HEADS * CHUNK, CHUNK), const),
        pl.BlockSpec((CHUNK, aw), const),
        pl.BlockSpec((1, bw), const),
        pl.BlockSpec((1, bw), const),
        pl.BlockSpec((V7X_MXU_DIM, V7X_MXU_DIM), const),
    ]
    args = [x, mod, g1, w_in, lng, lnb, ws_all, bs_full, gq, gk, gsum]
    if rope:
        in_specs += [pl.BlockSpec((tm, 128), lambda b, i: (i, 0))] * 3
        args += list(rope_tabs)
    widths = (aw, bw, bw, bw, aw)
    return pl.pallas_call(
        functools.partial(_proj_kernel, rope=rope, D=D),
        grid=(nb, n // tm),
        in_specs=in_specs,
        out_specs=[pl.BlockSpec((1, tm, w), tok) for w in widths],
        out_shape=[jax.ShapeDtypeStruct((nb, n, w), BF16) for w in widths],
        compiler_params=_cparams(("parallel", "parallel")),
        name="proj_rope" if rope else "proj",
    )(*args)


def _attn_kernel(lam_ref, q_ref, *refs, nseg):
    kv = refs[:2 * nseg]
    g_ref, o_ref = refs[2 * nseg], refs[2 * nseg + 1]
    q = q_ref[0]
    tq = q.shape[0]
    lane = lax.broadcasted_iota(jnp.int32, q.shape, 1)
    zero = jnp.zeros_like(q)
    q2 = jnp.concatenate([jnp.where(lane < B_QK_DIM, q, zero), jnp.where(lane >= B_QK_DIM, q, zero)], axis=0)
    ss = [lax.dot_general(q2, kv[2 * s][0], (((1,), (1,)), ((), ())), preferred_element_type=F32)
          for s in range(nseg)]
    m = ss[0].max(axis=-1, keepdims=True)
    for s in ss[1:]:
        m = jnp.maximum(m, s.max(axis=-1, keepdims=True))
    acc = None
    den = None
    for si, s in enumerate(ss):
        e = jnp.exp(s - m)
        d = e.sum(axis=-1, keepdims=True)
        o = jnp.dot(e.astype(BF16), kv[2 * si + 1][0], preferred_element_type=F32)
        acc = o if acc is None else acc + o
        den = d if den is None else den + d
    o2 = acc / den
    o = o2[:tq] - lam_ref[0] * o2[tq:]
    o_ref[0] = _rms(o, g_ref[...]).astype(BF16)


def _attn(lam, q, segs, g_sub, tq):
    nb, n, bw = q.shape
    in_specs = [
        pl.BlockSpec(memory_space=pltpu.SMEM),
        pl.BlockSpec((1, tq, B_V_DIM), lambda b, h, i: (b, i, h)),
    ]
    args = [lam, q]
    for (k, v) in segs:
        lk = k.shape[1]
        spec = pl.BlockSpec((1, lk, B_V_DIM), lambda b, h, i: (b, 0, h))
        in_specs += [spec, spec]
        args += [k, v]
    in_specs.append(pl.BlockSpec((1, B_V_DIM), lambda b, h, i: (0, 0)))
    args.append(g_sub)
    return pl.pallas_call(
        functools.partial(_attn_kernel, nseg=len(segs)),
        grid=(nb, B_HEADS, n // tq),
        in_specs=in_specs,
        out_specs=pl.BlockSpec((1, tq, B_V_DIM), lambda b, h, i: (b, i, h)),
        out_shape=jax.ShapeDtypeStruct((nb, n, bw), BF16),
        compiler_params=_cparams(("parallel", "parallel", "parallel")),
        name="attn%d" % len(segs),
    )(*args)


def _merge_kernel(x_ref, mod_ref, ya_ref, yb_ref, glu_ref, glup_ref, glun_ref, cw_ref, cb_ref,
                  lcg_ref, lcb_ref, wo_ref, o_ref, ext_ref, *, D):
    i = pl.program_id(1)
    nt = pl.num_programs(1)
    tm = x_ref.shape[1]
    prev = glup_ref[0].astype(F32)
    nxt = glun_ref[0].astype(F32)
    ext_ref[0:HALO, :] = jnp.where(i > 0, prev, jnp.zeros_like(prev))
    ext_ref[HALO:HALO + tm, :] = glu_ref[0].astype(F32)
    ext_ref[HALO + tm:, :] = jnp.where(i < nt - 1, nxt, jnp.zeros_like(nxt))
    base = HALO - C_CONV // 2
    acc = jnp.zeros((tm, ext_ref.shape[1]), F32) + cb_ref[...]
    for j in range(C_CONV):
        acc = acc + cw_ref[j:j + 1, :] * ext_ref[base + j:base + j + tm, :]
    yc = _silu(_ln(acc, lcg_ref[...], lcb_ref[...]))
    mix = jnp.concatenate([ya_ref[0], yb_ref[0], yc.astype(BF16)], axis=-1)
    out = jnp.dot(mix, wo_ref[...], preferred_element_type=F32)
    gate = mod_ref[0][:, 2 * D:3 * D]
    o_ref[0] = x_ref[0] + gate * out


def _merge(x, mod, ya, yb, glu, cw, cb, lcg, lcb, w_out, tm):
    nb, n, D = x.shape
    aw, bw = D // 4, D // 2
    r = tm // HALO
    nh = n // HALO
    const = lambda b, i: (0, 0)
    tok = lambda b, i: (b, i, 0)
    return pl.pallas_call(
        functools.partial(_merge_kernel, D=D),
        grid=(nb, n // tm),
        in_specs=[
            pl.BlockSpec((1, tm, D), tok),
            pl.BlockSpec((1, 1, 6 * D), lambda b, i: (b, 0, 0)),
            pl.BlockSpec((1, tm, aw), tok),
            pl.BlockSpec((1, tm, bw), tok),
            pl.BlockSpec((1, tm, aw), tok),
            pl.BlockSpec((1, HALO, aw), lambda b, i: (b, jnp.maximum(i * r - 1, 0), 0)),
            pl.BlockSpec((1, HALO, aw), lambda b, i: (b, jnp.minimum((i + 1) * r, nh - 1), 0)),
            pl.BlockSpec((C_CONV, aw), const),
            pl.BlockSpec((1, aw), const),
            pl.BlockSpec((1, aw), const),
            pl.BlockSpec((1, aw), const),
            pl.BlockSpec((D, D), const),
        ],
        out_specs=pl.BlockSpec((1, tm, D), tok),
        out_shape=jax.ShapeDtypeStruct((nb, n, D), F32),
        scratch_shapes=[pltpu.VMEM((tm + 2 * HALO, aw), F32)],
        compiler_params=_cparams(("parallel", "parallel")),
        name="merge",
    )(x, mod, ya, yb, glu, glu, glu, cw, cb, lcg, lcb, w_out)


def _ffn_chunks(F):
    step = 4 * V7X_MXU_DIM
    return [(c, min(c + step, F)) for c in range(0, F, step)]


def _ffn_kernel(x_ref, xp_ref, xn_ref, mod_ref, g2_ref, wg_ref, wv_ref, wd_ref, cw_ref, cb_ref,
                o_ref, gt_ref, *, D):
    i = pl.program_id(1)
    nt = pl.num_programs(1)
    tm = x_ref.shape[1]
    F = wg_ref.shape[1]
    mod = mod_ref[0]
    shift, scale, gate = mod[:, 3 * D:4 * D], mod[:, 4 * D:5 * D], mod[:, 5 * D:6 * D]

    def h2(xr):
        return (_rms(xr, g2_ref[...]) * (1.0 + scale) + shift).astype(BF16)

    x = x_ref[0]
    hp = h2(xp_ref[0])
    hn = h2(xn_ref[0])
    hp = jnp.where(i > 0, hp, jnp.zeros_like(hp))
    hn = jnp.where(i < nt - 1, hn, jnp.zeros_like(hn))
    hm = h2(x)
    hext = jnp.concatenate([hp, hm, hn], axis=0)
    acc = None
    for (c0, c1) in _ffn_chunks(F):
        w = c1 - c0
        gt_ref[:, 0:w] = jnp.dot(hext, wg_ref[:, c0:c1], preferred_element_type=F32)
        cw = cw_ref[:, c0:c1]
        conv = (cw[0:1] * gt_ref[HALO - 1:HALO - 1 + tm, 0:w] + cw[1:2] * gt_ref[HALO:HALO + tm, 0:w]
                + cw[2:3] * gt_ref[HALO + 1:HALO + 1 + tm, 0:w] + cb_ref[:, c0:c1])
        val = jnp.dot(hm, wv_ref[:, c0:c1], preferred_element_type=F32)
        a = (_silu(conv) * val).astype(BF16)
        o = jnp.dot(a, wd_ref[c0:c1, :], preferred_element_type=F32)
        acc = o if acc is None else acc + o
    o_ref[0] = x + gate * acc


def _ffn(x, mod, g2, wg, wv, wd, cw, cb, tm):
    nb, n, D = x.shape
    F = wg.shape[1]
    r = tm // HALO
    nh = n // HALO
    const = lambda b, i: (0, 0)
    tok = lambda b, i: (b, i, 0)
    resident = dict(pipeline_mode=pl.Buffered(1))
    wmax = max(c1 - c0 for c0, c1 in _ffn_chunks(F))
    return pl.pallas_call(
        functools.partial(_ffn_kernel, D=D),
        grid=(nb, n // tm),
        in_specs=[
            pl.BlockSpec((1, tm, D), tok),
            pl.BlockSpec((1, HALO, D), lambda b, i: (b, jnp.maximum(i * r - 1, 0), 0)),
            pl.BlockSpec((1, HALO, D), lambda b, i: (b, jnp.minimum((i + 1) * r, nh - 1), 0)),
            pl.BlockSpec((1, 1, 6 * D), lambda b, i: (b, 0, 0)),
            pl.BlockSpec((1, D), const),
            pl.BlockSpec((D, F), const, **resident),
            pl.BlockSpec((D, F), const, **resident),
            pl.BlockSpec((F, D), const, **resident),
            pl.BlockSpec((FFN_CONV, F), const),
            pl.BlockSpec((1, F), const),
        ],
        out_specs=pl.BlockSpec((1, tm, D), tok),
        out_shape=jax.ShapeDtypeStruct((nb, n, D), F32),
        scratch_shapes=[pltpu.VMEM((tm + 2 * HALO, wmax), F32)],
        compiler_params=_cparams(("parallel", "parallel")),
        name="ffn",
    )(x, x, x, mod, g2, wg, wv, wd, cw, cb)


def _rope_tables(n):
    t = jnp.arange(n)
    pos = jnp.stack([(t // GRID_W).astype(F32), (t % GRID_W).astype(F32)], axis=1)
    half_dim = B_QK_DIM // 2
    inv = ROPE_THETA ** (-jnp.arange(0, half_dim, 2, dtype=F32) / half_dim)
    lane = jnp.arange(128)
    g64 = lane % B_QK_DIM
    axis = g64 // half_dim
    second = (g64 % half_dim) // (half_dim // 2)
    ang = pos[:, axis] * inv[g64 % (half_dim // 2)][None, :]
    cos, sin = jnp.cos(ang), jnp.sin(ang)
    zero = jnp.zeros_like(sin)
    return cos, jnp.where(second[None, :] == 0, -sin, zero), jnp.where(second[None, :] == 1, sin, zero)


def kernel(x, c, ctx, c_ctx, w_mod, b_mod, norm1_g, w_in, ln_v_g, ln_v_b, w_s, b_s, q_norm_g, k_norm_g,
           lam_q1, lam_k1, lam_q2, lam_k2, subln_g, conv_w, conv_b, ln_c_g, ln_c_b, w_out, norm2_g,
           w_gate, w_val, ffn_conv_w, ffn_conv_b, w_down):
    nb, n, D = x.shape
    n_ctx = ctx.shape[1]
    L = w_mod.shape[0]
    aw = D // 4
    assert nb + 1 <= MOD_ROWS and n % GRID_W == 0

    cvec = jnp.zeros((MOD_ROWS, D), F32).at[:nb].set(c).at[nb].set(c_ctx)
    mods, lam_dyn = _adaln(cvec, w_mod, b_mod, lam_q1, lam_k1, lam_q2, lam_k2)

    tm = min(512, n)
    tm_c = min(512, n_ctx)
    tq = min(256, n)
    tq_c = min(256, n_ctx)
    rope_tabs = _rope_tables(n)
    gsum = jnp.kron(jnp.eye(V7X_MXU_DIM // B_QK_DIM, dtype=F32), jnp.ones((B_QK_DIM, B_QK_DIM), F32)).astype(BF16)
    reps = (D // 2) // B_QK_DIM

    x_lat, x_ctx = x, ctx
    for l in range(L):
        last = l == L - 1
        lam_init = 0.8 - 0.6 * math.exp(-0.3 * l)
        lam = lam_dyn[l, 0, 0:1] + lam_init
        mod_l = mods[l, :nb].reshape(nb, 1, 6 * D)
        mod_c = jnp.broadcast_to(mods[l, nb].reshape(1, 1, 6 * D), (nb, 1, 6 * D))
        row = lambda a: a[l].reshape(1, -1)
        w_in_b = w_in[l].astype(BF16)
        ws_all = w_s[l].reshape(A_HEADS * CHUNK, CHUNK).astype(BF16)
        bs_full = jnp.repeat(b_s[l].T, aw // A_HEADS, axis=1)
        gq = jnp.tile(q_norm_g[l], reps).reshape(1, -1) * (B_QK_DIM ** -0.5)
        gk = jnp.tile(k_norm_g[l], reps).reshape(1, -1)
        g_sub = row(subln_g) * (1.0 - lam_init)
        proj_args = (row(norm1_g), w_in_b, row(ln_v_g), row(ln_v_b), ws_all, bs_full, gq, gk, gsum)
        merge_args = (conv_w[l], row(conv_b), row(ln_c_g), row(ln_c_b), w_out[l].astype(BF16))
        ffn_args = (row(norm2_g), w_gate[l].astype(BF16), w_val[l].astype(BF16), w_down[l].astype(BF16),
                    ffn_conv_w[l], row(ffn_conv_b))

        ya_c, q_c, k_c, v_c, glu_c = _proj(x_ctx, mod_c, *proj_args, None, tm_c)
        if not last:
            yb_c = _attn(lam, q_c, [(k_c, v_c)], g_sub, tq_c)
            x_ctx = _merge(x_ctx, mod_c, ya_c, yb_c, glu_c, *merge_args, tm_c)
            x_ctx = _ffn(x_ctx, mod_c, *ffn_args, tm_c)

        ya, q, k, v, glu = _proj(x_lat, mod_l, *proj_args, rope_tabs, tm)
        yb = _attn(lam, q, [(k, v), (k_c, v_c)], g_sub, tq)
        x_lat = _merge(x_lat, mod_l, ya, yb, glu, *merge_args, tm)
        x_lat = _ffn(x_lat, mod_l, *ffn_args, tm)
    return x_lat
```

```python
import functools
import math

import jax
import jax.numpy as jnp
from jax import lax
from jax.experimental import pallas as pl
from jax.experimental.pallas import tpu as pltpu

F32 = jnp.float32
BF16 = jnp.bfloat16

EPS = 1e-6
GRID_W = 64
ROPE_THETA = 10000.0
A_HEADS = 4
CHUNK = 128
B_HEADS = 4
B_QK_DIM = 64
B_V_DIM = 128
C_CONV = 31
FFN_CONV = 3

V7X_MXU_DIM = 256
V7X_VMEM_LIMIT_BYTES = 56 * 1024 * 1024
BF16_SUBLANES = 16
HALO = 16
MOD_ROWS = 24
ATTN_KEY_CHUNK = 256
CONV_ROW_BLOCK = 64
ATTN_UNSHIFTED_MAX_SCORE = 64.0


def _cparams(sem):
    return pltpu.CompilerParams(dimension_semantics=sem, vmem_limit_bytes=V7X_VMEM_LIMIT_BYTES)


def _rms(x, g):
    return x * lax.rsqrt(jnp.mean(x * x, axis=-1, keepdims=True) + EPS) * g


def _ln(x, g, b):
    xc = x - jnp.mean(x, axis=-1, keepdims=True)
    return xc * lax.rsqrt(jnp.mean(xc * xc, axis=-1, keepdims=True) + EPS) * g + b


def _silu(x):
    return x * jax.nn.sigmoid(x)


def _adaln_kernel(c_ref, w_ref, b_ref, q1_ref, k1_ref, q2_ref, k2_ref, o_ref, lam_ref):
    s = _silu(c_ref[...])
    o_ref[0] = jnp.dot(s.astype(BF16), w_ref[0].astype(BF16), preferred_element_type=F32) + b_ref[0]
    d1 = jnp.sum(q1_ref[0] * k1_ref[0], axis=-1, keepdims=True)
    d2 = jnp.sum(q2_ref[0] * k2_ref[0], axis=-1, keepdims=True)
    lam_ref[0] = jnp.broadcast_to(jnp.exp(d1) - jnp.exp(d2), lam_ref.shape[1:])


def _adaln(cvec, w_mod, b_mod, lq1, lk1, lq2, lk2):
    L, D, N6 = w_mod.shape
    tn = N6 // 4
    small = pl.BlockSpec((1, 1, B_QK_DIM), lambda l, j: (l, 0, 0))
    return pl.pallas_call(
        _adaln_kernel,
        grid=(L, N6 // tn),
        in_specs=[
            pl.BlockSpec((MOD_ROWS, D), lambda l, j: (0, 0)),
            pl.BlockSpec((1, D, tn), lambda l, j: (l, 0, j)),
            pl.BlockSpec((1, 1, tn), lambda l, j: (l, 0, j)),
            small, small, small, small,
        ],
        out_specs=[
            pl.BlockSpec((1, MOD_ROWS, tn), lambda l, j: (l, 0, j)),
            pl.BlockSpec((1, 8, 128), lambda l, j: (l, 0, 0)),
        ],
        out_shape=[
            jax.ShapeDtypeStruct((L, MOD_ROWS, N6), F32),
            jax.ShapeDtypeStruct((L, 8, 128), F32),
        ],
        compiler_params=_cparams(("arbitrary", "arbitrary")),
        name="adaln",
    )(cvec, w_mod, b_mod.reshape(L, 1, N6), lq1.reshape(L, 1, -1), lk1.reshape(L, 1, -1),
      lq2.reshape(L, 1, -1), lk2.reshape(L, 1, -1))


def _proj_kernel(*refs, rope, D):
    (x_ref, mod_ref, g1_ref, w_ref, lng_ref, lnb_ref, ws_ref, bs_ref, gq_ref, gk_ref, gsum_ref) = refs[:11]
    if rope:
        cos_ref, sa_ref, sb_ref = refs[11:14]
        outs = refs[14:]
    else:
        outs = refs[11:]
    ya_ref, q_ref, k_ref, v_ref, glu_ref = outs

    aw = D // 4
    bw = D // 2
    x = x_ref[0]
    tm = x.shape[0]
    mod = mod_ref[0]
    shift, scale = mod[:, 0:D], mod[:, D:2 * D]
    h = (_rms(x, g1_ref[...]) * (1.0 + scale) + shift).astype(BF16)
    c_q = 2 * aw
    c_k = c_q + bw
    c_v = c_k + bw
    c_c = c_v + bw

    def cols(c0, c1):
        return jnp.dot(h, w_ref[:, c0:c1], preferred_element_type=F32)

    def epilogue_a(pa):
        z = jax.nn.gelu(pa)
        u, v = z[:, :aw], z[:, aw:]
        vn = _ln(v, lng_ref[...], lnb_ref[...]).astype(BF16)
        head = lax.broadcasted_iota(jnp.int32, (CHUNK, aw), 1) // (aw // A_HEADS)
        ws = ws_ref[...]
        for c in range(tm // CHUNK):
            r = jnp.dot(ws, vn[c * CHUNK:(c + 1) * CHUNK, :], preferred_element_type=F32)
            gate = r[0:CHUNK]
            for hh in range(1, A_HEADS):
                gate = jnp.where(head == hh, r[hh * CHUNK:(hh + 1) * CHUNK], gate)
            gate = gate + bs_ref[...]
            ya_ref[0, c * CHUNK:(c + 1) * CHUNK, :] = (u[c * CHUNK:(c + 1) * CHUNK] * gate).astype(BF16)

    def qk_norm(t, g):
        sq = (t * t).astype(BF16)
        parts = [jnp.dot(sq[:, j:j + V7X_MXU_DIM], gsum_ref[...], preferred_element_type=F32)
                 for j in range(0, bw, V7X_MXU_DIM)]
        ss = jnp.concatenate(parts, axis=-1)
        t = t * lax.rsqrt(ss * (1.0 / B_QK_DIM) + EPS) * g
        if not rope:
            return t
        cos, sa, sb = cos_ref[...], sa_ref[...], sb_ref[...]
        blocks = []
        for j in range(0, bw, 128):
            tb = t[:, j:j + 128]
            blocks.append(tb * cos + pltpu.roll(tb, 128 - 16, 1) * sa + pltpu.roll(tb, 16, 1) * sb)
        return jnp.concatenate(blocks, axis=-1)

    pa = cols(0, c_q)
    pq = cols(c_q, c_k)
    epilogue_a(pa)
    pk = cols(c_k, c_v)
    q_ref[0] = qk_norm(pq, gq_ref[...]).astype(BF16)
    pv = cols(c_v, c_c)
    k_ref[0] = qk_norm(pk, gk_ref[...]).astype(BF16)
    pc = cols(c_c, c_c + 2 * aw)
    v_ref[0] = pv.astype(BF16)
    glu_ref[0] = (pc[:, :aw] * jax.nn.sigmoid(pc[:, aw:])).astype(BF16)


def _proj(x, mod, g1, w_in, lng, lnb, ws_all, bs_full, gq, gk, gsum, rope_tabs, tm):
    nb, n, D = x.shape
    aw, bw = D // 4, D // 2
    cols = w_in.shape[1]
    rope = rope_tabs is not None
    const = lambda b, i: (0, 0)
    tok = lambda b, i: (b, i, 0)
    in_specs = [
        pl.BlockSpec((1, tm, D), tok),
        pl.BlockSpec((1, 1, 6 * D), lambda b, i: (b, 0, 0)),
        pl.BlockSpec((1, D), const),
        pl.BlockSpec((D, cols), const),
        pl.BlockSpec((1, aw), const),
        pl.BlockSpec((1, aw), const),
        pl.BlockSpec((A_HEADS * CHUNK, CHUNK), const),
        pl.BlockSpec((CHUNK, aw), const),
        pl.BlockSpec((1, bw), const),
        pl.BlockSpec((1, bw), const),
        pl.BlockSpec((V7X_MXU_DIM, V7X_MXU_DIM), const),
    ]
    args = [x, mod, g1, w_in, lng, lnb, ws_all, bs_full, gq, gk, gsum]
    if rope:
        in_specs += [pl.BlockSpec((tm, 128), lambda b, i: (i, 0))] * 3
        args += list(rope_tabs)
    widths = (aw, bw, bw, bw, aw)
    return pl.pallas_call(
        functools.partial(_proj_kernel, rope=rope, D=D),
        grid=(nb, n // tm),
        in_specs=in_specs,
        out_specs=[pl.BlockSpec((1, tm, w), tok) for w in widths],
        out_shape=[jax.ShapeDtypeStruct((nb, n, w), BF16) for w in widths],
        compiler_params=_cparams(("parallel", "parallel")),
        name="proj_rope" if rope else "proj",
    )(*args)


def _stack_maps(q):
    lane = lax.broadcasted_iota(jnp.int32, q.shape, 1)
    zero = jnp.zeros_like(q)
    return jnp.concatenate([jnp.where(lane < B_QK_DIM, q, zero), jnp.where(lane >= B_QK_DIM, q, zero)], axis=0)


def _attn_finish(acc, den, lam, g_ref, o_ref):
    tq = acc.shape[0] // 2
    o2 = acc / den
    o = o2[:tq] - lam * o2[tq:]
    o_ref[0] = _rms(o, g_ref[...]).astype(BF16)


_NT = (((1,), (1,)), ((), ()))


def _attn_kernel(lam_ref, q_ref, *refs, nseg, shift):
    kv = refs[:2 * nseg]
    g_ref, o_ref = refs[2 * nseg], refs[2 * nseg + 1]
    q2 = _stack_maps(q_ref[0])
    if shift:
        ss = [lax.dot_general(q2, kv[2 * s][0], _NT, preferred_element_type=F32) for s in range(nseg)]
        m = ss[0].max(axis=-1, keepdims=True)
        for s in ss[1:]:
            m = jnp.maximum(m, s.max(axis=-1, keepdims=True))
        acc = None
        den = None
        for si, s in enumerate(ss):
            e = jnp.exp2(s - m)
            d = e.sum(axis=-1, keepdims=True)
            o = jnp.dot(e.astype(BF16), kv[2 * si + 1][0], preferred_element_type=F32)
            acc = o if acc is None else acc + o
            den = d if den is None else den + d
    else:
        acc = None
        part = None
        for si in range(nseg):
            k_ref, v_ref = kv[2 * si], kv[2 * si + 1]
            lk = k_ref.shape[1]
            kc = min(ATTN_KEY_CHUNK, lk)
            for c0 in range(0, lk, kc):
                s = lax.dot_general(q2, k_ref[0, c0:c0 + kc, :], _NT, preferred_element_type=F32)
                e = jnp.exp2(s)
                for j in range(0, kc, 128):
                    part = e[:, j:j + 128] if part is None else part + e[:, j:j + 128]
                o = jnp.dot(e.astype(BF16), v_ref[0, c0:c0 + kc, :], preferred_element_type=F32)
                acc = o if acc is None else acc + o
        den = part.sum(axis=-1, keepdims=True)
    _attn_finish(acc, den, lam_ref[0], g_ref, o_ref)


def _attn(lam, q, segs, g_sub, tq, shift):
    nb, n, bw = q.shape
    in_specs = [
        pl.BlockSpec(memory_space=pltpu.SMEM),
        pl.BlockSpec((1, tq, B_V_DIM), lambda b, h, i: (b, i, h)),
    ]
    args = [lam, q]
    for (k, v) in segs:
        lk = k.shape[1]
        spec = pl.BlockSpec((1, lk, B_V_DIM), lambda b, h, i: (b, 0, h))
        in_specs += [spec, spec]
        args += [k, v]
    in_specs.append(pl.BlockSpec((1, B_V_DIM), lambda b, h, i: (0, 0)))
    args.append(g_sub)
    return pl.pallas_call(
        functools.partial(_attn_kernel, nseg=len(segs), shift=shift),
        grid=(nb, B_HEADS, n // tq),
        in_specs=in_specs,
        out_specs=pl.BlockSpec((1, tq, B_V_DIM), lambda b, h, i: (b, i, h)),
        out_shape=jax.ShapeDtypeStruct((nb, n, bw), BF16),
        compiler_params=_cparams(("parallel", "parallel", "parallel")),
        name="attn%d%s" % (len(segs), "_shift" if shift else ""),
    )(*args)


def _attn_guarded(score_bound, lam, q, segs, g_sub, tq, tq_shift):
    flat = [a for kv in segs for a in kv]

    def run(shift, lam, q, g_sub, *flat):
        segs = [(flat[2 * s], flat[2 * s + 1]) for s in range(len(flat) // 2)]
        return _attn(lam, q, segs, g_sub, tq_shift if shift else tq, shift)

    return lax.cond(score_bound <= ATTN_UNSHIFTED_MAX_SCORE,
                    functools.partial(run, False), functools.partial(run, True), lam, q, g_sub, *flat)


def _merge_kernel(x_ref, mod_ref, ya_ref, yb_ref, glu_ref, glup_ref, glun_ref, cw_ref, cb_ref,
                  lcg_ref, lcb_ref, wo_ref, o_ref, ext_ref, sh_ref, yc_ref, *, D):
    i = pl.program_id(1)
    nt = pl.num_programs(1)
    tm = x_ref.shape[1]
    prev = glup_ref[0].astype(F32)
    nxt = glun_ref[0].astype(F32)
    ext_ref[0:HALO, :] = jnp.where(i > 0, prev, jnp.zeros_like(prev))
    ext_ref[HALO:HALO + tm, :] = glu_ref[0].astype(F32)
    ext_ref[HALO + tm:, :] = jnp.where(i < nt - 1, nxt, jnp.zeros_like(nxt))
    base = HALO - C_CONV // 2
    srcs = {}
    for r in range(8):
        off = base + r
        n_taps = len(range(r, C_CONV, 8))
        if off % 8 == 0:
            srcs[r] = (ext_ref, off)
        else:
            span = tm + 8 * (n_taps - 1)
            sh_ref[r, 0:span, :] = ext_ref[off:off + span, :]
            srcs[r] = (sh_ref.at[r], 0)
    for rb in range(0, tm, CONV_ROW_BLOCK):
        acc = jnp.broadcast_to(cb_ref[...], (CONV_ROW_BLOCK, cb_ref.shape[1]))
        for j in range(C_CONV):
            src, off = srcs[j % 8]
            r0 = off + 8 * (j // 8) + rb
            acc = acc + cw_ref[j:j + 1, :] * src[r0:r0 + CONV_ROW_BLOCK, :]
        yc_ref[rb:rb + CONV_ROW_BLOCK, :] = acc
    yc = _silu(_ln(yc_ref[...], lcg_ref[...], lcb_ref[...]))
    mix = jnp.concatenate([ya_ref[0], yb_ref[0], yc.astype(BF16)], axis=-1)
    out = jnp.dot(mix, wo_ref[...], preferred_element_type=F32)
    gate = mod_ref[0][:, 2 * D:3 * D]
    o_ref[0] = x_ref[0] + gate * out


def _merge(x, mod, ya, yb, glu, cw, cb, lcg, lcb, w_out, tm):
    nb, n, D = x.shape
    aw, bw = D // 4, D // 2
    r = tm // HALO
    nh = n // HALO
    const = lambda b, i: (0, 0)
    tok = lambda b, i: (b, i, 0)
    return pl.pallas_call(
        functools.partial(_merge_kernel, D=D),
        grid=(nb, n // tm),
        in_specs=[
            pl.BlockSpec((1, tm, D), tok),
            pl.BlockSpec((1, 1, 6 * D), lambda b, i: (b, 0, 0)),
            pl.BlockSpec((1, tm, aw), tok),
            pl.BlockSpec((1, tm, bw), tok),
            pl.BlockSpec((1, tm, aw), tok),
            pl.BlockSpec((1, HALO, aw), lambda b, i: (b, jnp.maximum(i * r - 1, 0), 0)),
            pl.BlockSpec((1, HALO, aw), lambda b, i: (b, jnp.minimum((i + 1) * r, nh - 1), 0)),
            pl.BlockSpec((C_CONV, aw), const),
            pl.BlockSpec((1, aw), const),
            pl.BlockSpec((1, aw), const),
            pl.BlockSpec((1, aw), const),
            pl.BlockSpec((D, D), const),
        ],
        out_specs=pl.BlockSpec((1, tm, D), tok),
        out_shape=jax.ShapeDtypeStruct((nb, n, D), F32),
        scratch_shapes=[pltpu.VMEM((tm + 2 * HALO, aw), F32),
                        pltpu.VMEM((8, tm + 8 * (C_CONV // 8), aw), F32),
                        pltpu.VMEM((tm, aw), F32)],
        compiler_params=_cparams(("parallel", "parallel")),
        name="merge",
    )(x, mod, ya, yb, glu, glu, glu, cw, cb, lcg, lcb, w_out)


def _ffn_chunks(F):
    step = 4 * V7X_MXU_DIM
    return [(c, min(c + step, F)) for c in range(0, F, step)]


def _ffn_kernel(x_ref, xp_ref, xn_ref, mod_ref, g2_ref, wg_ref, wv_ref, wd_ref, cw_ref, cb_ref,
                o_ref, gt_ref, *, D):
    i = pl.program_id(1)
    nt = pl.num_programs(1)
    tm = x_ref.shape[1]
    F = wg_ref.shape[1]
    mod = mod_ref[0]
    shift, scale, gate = mod[:, 3 * D:4 * D], mod[:, 4 * D:5 * D], mod[:, 5 * D:6 * D]

    def h2(xr):
        return (_rms(xr, g2_ref[...]) * (1.0 + scale) + shift).astype(BF16)

    x = x_ref[0]
    hp = h2(xp_ref[0])
    hn = h2(xn_ref[0])
    hp = jnp.where(i > 0, hp, jnp.zeros_like(hp))
    hn = jnp.where(i < nt - 1, hn, jnp.zeros_like(hn))
    hm = h2(x)
    hext = jnp.concatenate([hp, hm, hn], axis=0)
    chunks = _ffn_chunks(F)

    def up(c0, c1):
        gt_ref[:, c0:c1] = jnp.dot(hext, wg_ref[:, c0:c1], preferred_element_type=F32)
        return jnp.dot(hm, wv_ref[:, c0:c1], preferred_element_type=F32)

    def act(c0, c1, val):
        cw = cw_ref[:, c0:c1]
        conv = (cw[0:1] * gt_ref[HALO - 1:HALO - 1 + tm, c0:c1] + cw[1:2] * gt_ref[HALO:HALO + tm, c0:c1]
                + cw[2:3] * gt_ref[HALO + 1:HALO + 1 + tm, c0:c1] + cb_ref[:, c0:c1])
        return (_silu(conv) * val).astype(BF16)

    acc = None
    val = up(*chunks[0])
    for ci, (c0, c1) in enumerate(chunks):
        val_next = up(*chunks[ci + 1]) if ci + 1 < len(chunks) else None
        o = jnp.dot(act(c0, c1, val), wd_ref[c0:c1, :], preferred_element_type=F32)
        acc = o if acc is None else acc + o
        val = val_next
    o_ref[0] = x + gate * acc


def _ffn(x, mod, g2, wg, wv, wd, cw, cb, tm):
    nb, n, D = x.shape
    F = wg.shape[1]
    r = tm // HALO
    nh = n // HALO
    const = lambda b, i: (0, 0)
    tok = lambda b, i: (b, i, 0)
    resident = dict(pipeline_mode=pl.Buffered(1))
    return pl.pallas_call(
        functools.partial(_ffn_kernel, D=D),
        grid=(nb, n // tm),
        in_specs=[
            pl.BlockSpec((1, tm, D), tok),
            pl.BlockSpec((1, HALO, D), lambda b, i: (b, jnp.maximum(i * r - 1, 0), 0)),
            pl.BlockSpec((1, HALO, D), lambda b, i: (b, jnp.minimum((i + 1) * r, nh - 1), 0)),
            pl.BlockSpec((1, 1, 6 * D), lambda b, i: (b, 0, 0)),
            pl.BlockSpec((1, D), const),
            pl.BlockSpec((D, F), const, **resident),
            pl.BlockSpec((D, F), const, **resident),
            pl.BlockSpec((F, D), const, **resident),
            pl.BlockSpec((FFN_CONV, F), const),
            pl.BlockSpec((1, F), const),
        ],
        out_specs=pl.BlockSpec((1, tm, D), tok),
        out_shape=jax.ShapeDtypeStruct((nb, n, D), F32),
        scratch_shapes=[pltpu.VMEM((tm + 2 * HALO, F), F32)],
        compiler_params=_cparams(("parallel", "parallel")),
        name="ffn",
    )(x, x, x, mod, g2, wg, wv, wd, cw, cb)


def _rope_tables(n):
    t = jnp.arange(n)
    pos = jnp.stack([(t // GRID_W).astype(F32), (t % GRID_W).astype(F32)], axis=1)
    half_dim = B_QK_DIM // 2
    inv = ROPE_THETA ** (-jnp.arange(0, half_dim, 2, dtype=F32) / half_dim)
    lane = jnp.arange(128)
    g64 = lane % B_QK_DIM
    axis = g64 // half_dim
    second = (g64 % half_dim) // (half_dim // 2)
    ang = pos[:, axis] * inv[g64 % (half_dim // 2)][None, :]
    cos, sin = jnp.cos(ang), jnp.sin(ang)
    zero = jnp.zeros_like(sin)
    return cos, jnp.where(second[None, :] == 0, -sin, zero), jnp.where(second[None, :] == 1, sin, zero)


def kernel(x, c, ctx, c_ctx, w_mod, b_mod, norm1_g, w_in, ln_v_g, ln_v_b, w_s, b_s, q_norm_g, k_norm_g,
           lam_q1, lam_k1, lam_q2, lam_k2, subln_g, conv_w, conv_b, ln_c_g, ln_c_b, w_out, norm2_g,
           w_gate, w_val, ffn_conv_w, ffn_conv_b, w_down):
    nb, n, D = x.shape
    n_ctx = ctx.shape[1]
    L = w_mod.shape[0]
    aw = D // 4
    assert nb + 1 <= MOD_ROWS and n % GRID_W == 0

    cvec = jnp.zeros((MOD_ROWS, D), F32).at[:nb].set(c).at[nb].set(c_ctx)
    mods, lam_dyn = _adaln(cvec, w_mod, b_mod, lam_q1, lam_k1, lam_q2, lam_k2)

    tm = min(512, n)
    tm_c = min(512, n_ctx)
    tq = min(2048, n)
    tq_c = min(256, n_ctx)
    rope_tabs = _rope_tables(n)
    gsum = jnp.kron(jnp.eye(V7X_MXU_DIM // B_QK_DIM, dtype=F32), jnp.ones((B_QK_DIM, B_QK_DIM), F32)).astype(BF16)
    reps = (D // 2) // B_QK_DIM

    x_lat, x_ctx = x, ctx
    for l in range(L):
        last = l == L - 1
        lam_init = 0.8 - 0.6 * math.exp(-0.3 * l)
        lam = lam_dyn[l, 0, 0:1] + lam_init
        mod_l = mods[l, :nb].reshape(nb, 1, 6 * D)
        mod_c = jnp.broadcast_to(mods[l, nb].reshape(1, 1, 6 * D), (nb, 1, 6 * D))
        row = lambda a: a[l].reshape(1, -1)
        w_in_b = w_in[l].astype(BF16)
        ws_all = w_s[l].reshape(A_HEADS * CHUNK, CHUNK).astype(BF16)
        bs_full = jnp.repeat(b_s[l].T, aw // A_HEADS, axis=1)
        gq = jnp.tile(q_norm_g[l], reps).reshape(1, -1) * (B_QK_DIM ** -0.5 * math.log2(math.e))
        score_bound = 1.02 * B_QK_DIM ** 0.5 * jnp.max(jnp.abs(q_norm_g[l])) * jnp.max(jnp.abs(k_norm_g[l]))
        gk = jnp.tile(k_norm_g[l], reps).reshape(1, -1)
        g_sub = row(subln_g) * (1.0 - lam_init)
        proj_args = (row(norm1_g), w_in_b, row(ln_v_g), row(ln_v_b), ws_all, bs_full, gq, gk, gsum)
        merge_args = (conv_w[l], row(conv_b), row(ln_c_g), row(ln_c_b), w_out[l].astype(BF16))
        ffn_args = (row(norm2_g), w_gate[l].astype(BF16), w_val[l].astype(BF16), w_down[l].astype(BF16),
                    ffn_conv_w[l], row(ffn_conv_b))

        ya_c, q_c, k_c, v_c, glu_c = _proj(x_ctx, mod_c, *proj_args, None, tm_c)
        if not last:
            yb_c = _attn(lam, q_c, [(k_c, v_c)], g_sub, tq_c, True)
            x_ctx = _merge(x_ctx, mod_c, ya_c, yb_c, glu_c, *merge_args, tm_c)
            x_ctx = _ffn(x_ctx, mod_c, *ffn_args, tm_c)

        ya, q, k, v, glu = _proj(x_lat, mod_l, *proj_args, rope_tabs, tm)
        yb = _attn_guarded(score_bound, lam, q, [(k, v), (k_c, v_c)], g_sub, tq, tq_c)
        x_lat = _merge(x_lat, mod_l, ya, yb, glu, *merge_args, tm)
        x_lat = _ffn(x_lat, mod_l, *ffn_args, tm)
    return x_lat
```

```python
import functools
import math

import jax
import jax.numpy as jnp
from jax import lax
from jax.experimental import pallas as pl
from jax.experimental.pallas import tpu as pltpu

F32 = jnp.float32
BF16 = jnp.bfloat16

EPS = 1e-6
GRID_W = 64
ROPE_THETA = 10000.0
A_HEADS = 4
CHUNK = 128
B_HEADS = 4
B_QK_DIM = 64
B_V_DIM = 128
C_CONV = 31
FFN_CONV = 3

V7X_MXU_DIM = 256
V7X_VMEM_LIMIT_BYTES = 56 * 1024 * 1024
BF16_SUBLANES = 16
HALO = 16
MOD_ROWS = 24
ATTN_KEY_CHUNK = 512
CONV_ROW_BLOCK = 64
ATTN_UNSHIFTED_MAX_SCORE = 64.0


def _cparams(sem):
    return pltpu.CompilerParams(dimension_semantics=sem, vmem_limit_bytes=V7X_VMEM_LIMIT_BYTES)


def _rms(x, g):
    return x * lax.rsqrt(jnp.mean(x * x, axis=-1, keepdims=True) + EPS) * g


def _ln(x, g, b):
    xc = x - jnp.mean(x, axis=-1, keepdims=True)
    return xc * lax.rsqrt(jnp.mean(xc * xc, axis=-1, keepdims=True) + EPS) * g + b


def _silu(x):
    return x * jax.nn.sigmoid(x)


def _adaln_kernel(c_ref, w_ref, b_ref, q1_ref, k1_ref, q2_ref, k2_ref, o_ref, lam_ref):
    s = _silu(c_ref[...])
    o_ref[0] = jnp.dot(s.astype(BF16), w_ref[0].astype(BF16), preferred_element_type=F32) + b_ref[0]
    d1 = jnp.sum(q1_ref[0] * k1_ref[0], axis=-1, keepdims=True)
    d2 = jnp.sum(q2_ref[0] * k2_ref[0], axis=-1, keepdims=True)
    lam_ref[0] = jnp.broadcast_to(jnp.exp(d1) - jnp.exp(d2), lam_ref.shape[1:])


def _adaln(cvec, w_mod, b_mod, lq1, lk1, lq2, lk2):
    L, D, N6 = w_mod.shape
    tn = N6 // 4
    small = pl.BlockSpec((1, 1, B_QK_DIM), lambda l, j: (l, 0, 0))
    return pl.pallas_call(
        _adaln_kernel,
        grid=(L, N6 // tn),
        in_specs=[
            pl.BlockSpec((MOD_ROWS, D), lambda l, j: (0, 0)),
            pl.BlockSpec((1, D, tn), lambda l, j: (l, 0, j)),
            pl.BlockSpec((1, 1, tn), lambda l, j: (l, 0, j)),
            small, small, small, small,
        ],
        out_specs=[
            pl.BlockSpec((1, MOD_ROWS, tn), lambda l, j: (l, 0, j)),
            pl.BlockSpec((1, 8, 128), lambda l, j: (l, 0, 0)),
        ],
        out_shape=[
            jax.ShapeDtypeStruct((L, MOD_ROWS, N6), F32),
            jax.ShapeDtypeStruct((L, 8, 128), F32),
        ],
        compiler_params=_cparams(("arbitrary", "arbitrary")),
        name="adaln",
    )(cvec, w_mod, b_mod.reshape(L, 1, N6), lq1.reshape(L, 1, -1), lk1.reshape(L, 1, -1),
      lq2.reshape(L, 1, -1), lk2.reshape(L, 1, -1))


def _proj_kernel(*refs, rope, D):
    (x_ref, mod_ref, g1_ref, w_ref, lng_ref, lnb_ref, ws_ref, bs_ref, gq_ref, gk_ref, gsum_ref) = refs[:11]
    if rope:
        cos_ref, sa_ref, sb_ref = refs[11:14]
        outs = refs[14:]
    else:
        outs = refs[11:]
    ya_ref, q_ref, k_ref, vt_ref, glu_ref = outs

    aw = D // 4
    bw = D // 2
    x = x_ref[0]
    tm = x.shape[0]
    mod = mod_ref[0]
    shift, scale = mod[:, 0:D], mod[:, D:2 * D]
    h = (_rms(x, g1_ref[...]) * (1.0 + scale) + shift).astype(BF16)
    c_q = 2 * aw
    c_k = c_q + bw
    c_v = c_k + bw
    c_c = c_v + bw

    def cols(c0, c1):
        return jnp.dot(h, w_ref[:, c0:c1], preferred_element_type=F32)

    def epilogue_a(pa):
        z = jax.nn.gelu(pa)
        u, v = z[:, :aw], z[:, aw:]
        vn = _ln(v, lng_ref[...], lnb_ref[...]).astype(BF16)
        head = lax.broadcasted_iota(jnp.int32, (CHUNK, aw), 1) // (aw // A_HEADS)
        ws = ws_ref[...]
        for c in range(tm // CHUNK):
            r = jnp.dot(ws, vn[c * CHUNK:(c + 1) * CHUNK, :], preferred_element_type=F32)
            gate = r[0:CHUNK]
            for hh in range(1, A_HEADS):
                gate = jnp.where(head == hh, r[hh * CHUNK:(hh + 1) * CHUNK], gate)
            gate = gate + bs_ref[...]
            ya_ref[0, c * CHUNK:(c + 1) * CHUNK, :] = (u[c * CHUNK:(c + 1) * CHUNK] * gate).astype(BF16)

    def qk_norm(t, g):
        sq = (t * t).astype(BF16)
        parts = [jnp.dot(sq[:, j:j + V7X_MXU_DIM], gsum_ref[...], preferred_element_type=F32)
                 for j in range(0, bw, V7X_MXU_DIM)]
        ss = jnp.concatenate(parts, axis=-1)
        t = t * lax.rsqrt(ss * (1.0 / B_QK_DIM) + EPS) * g
        if not rope:
            return t
        cos, sa, sb = cos_ref[...], sa_ref[...], sb_ref[...]
        blocks = []
        for j in range(0, bw, 128):
            tb = t[:, j:j + 128]
            blocks.append(tb * cos + pltpu.roll(tb, 128 - 16, 1) * sa + pltpu.roll(tb, 16, 1) * sb)
        return jnp.concatenate(blocks, axis=-1)

    pa = cols(0, c_q)
    pq = cols(c_q, c_k)
    epilogue_a(pa)
    pk = cols(c_k, c_v)
    q_ref[0] = qk_norm(pq, gq_ref[...]).astype(BF16)
    pv = cols(c_v, c_c)
    k_ref[0] = qk_norm(pk, gk_ref[...]).astype(BF16)
    pc = cols(c_c, c_c + 2 * aw)
    vt_ref[0] = pv.astype(BF16).T
    glu_ref[0] = (pc[:, :aw] * jax.nn.sigmoid(pc[:, aw:])).astype(BF16)


def _proj(x, mod, g1, w_in, lng, lnb, ws_all, bs_full, gq, gk, gsum, rope_tabs, tm):
    nb, n, D = x.shape
    aw, bw = D // 4, D // 2
    cols = w_in.shape[1]
    rope = rope_tabs is not None
    const = lambda b, i: (0, 0)
    tok = lambda b, i: (b, i, 0)
    in_specs = [
        pl.BlockSpec((1, tm, D), tok),
        pl.BlockSpec((1, 1, 6 * D), lambda b, i: (b, 0, 0)),
        pl.BlockSpec((1, D), const),
        pl.BlockSpec((D, cols), const),
        pl.BlockSpec((1, aw), const),
        pl.BlockSpec((1, aw), const),
        pl.BlockSpec((A_HEADS * CHUNK, CHUNK), const),
        pl.BlockSpec((CHUNK, aw), const),
        pl.BlockSpec((1, bw), const),
        pl.BlockSpec((1, bw), const),
        pl.BlockSpec((V7X_MXU_DIM, V7X_MXU_DIM), const),
    ]
    args = [x, mod, g1, w_in, lng, lnb, ws_all, bs_full, gq, gk, gsum]
    if rope:
        in_specs += [pl.BlockSpec((tm, 128), lambda b, i: (i, 0))] * 3
        args += list(rope_tabs)
    tok_spec = lambda w: pl.BlockSpec((1, tm, w), tok)
    tok_shape = lambda w: jax.ShapeDtypeStruct((nb, n, w), BF16)
    return pl.pallas_call(
        functools.partial(_proj_kernel, rope=rope, D=D),
        grid=(nb, n // tm),
        in_specs=in_specs,
        out_specs=[tok_spec(aw), tok_spec(bw), tok_spec(bw),
                   pl.BlockSpec((1, bw, tm), lambda b, i: (b, 0, i)), tok_spec(aw)],
        out_shape=[tok_shape(aw), tok_shape(bw), tok_shape(bw),
                   jax.ShapeDtypeStruct((nb, bw, n), BF16), tok_shape(aw)],
        compiler_params=_cparams(("parallel", "parallel")),
        name="proj_rope" if rope else "proj",
    )(*args)


def _stack_maps(q):
    lane = lax.broadcasted_iota(jnp.int32, q.shape, 1)
    zero = jnp.zeros_like(q)
    return jnp.concatenate([jnp.where(lane < B_QK_DIM, q, zero), jnp.where(lane >= B_QK_DIM, q, zero)], axis=0)


def _attn_finish(acc_t, den, lam, g_ref, o_ref):
    tq = acc_t.shape[1] // 2
    o2 = acc_t / den
    o = o2[:, :tq] - lam * o2[:, tq:]
    z = o * lax.rsqrt(jnp.mean(o * o, axis=0, keepdims=True) + EPS)
    o_ref[0] = (z.T * g_ref[...]).astype(BF16)


_NT = (((1,), (1,)), ((), ()))


def _attn_kernel(lam_ref, q_ref, *refs, nseg, shift):
    kv = refs[:2 * nseg]
    g_ref, o_ref = refs[2 * nseg], refs[2 * nseg + 1]
    q2 = _stack_maps(q_ref[0])
    if shift:
        ss = [lax.dot_general(kv[2 * s][0], q2, _NT, preferred_element_type=F32) for s in range(nseg)]
        m = ss[0].max(axis=0, keepdims=True)
        for s in ss[1:]:
            m = jnp.maximum(m, s.max(axis=0, keepdims=True))
        acc = None
        den = None
        for si, s in enumerate(ss):
            e = jnp.exp2(s - m)
            d = e.sum(axis=0, keepdims=True)
            o = jnp.dot(kv[2 * si + 1][0], e.astype(BF16), preferred_element_type=F32)
            acc = o if acc is None else acc + o
            den = d if den is None else den + d
    else:
        acc = None
        part = None
        for si in range(nseg):
            k_ref, vt_ref = kv[2 * si], kv[2 * si + 1]
            lk = k_ref.shape[1]
            kc = min(ATTN_KEY_CHUNK, lk)
            for c0 in range(0, lk, kc):
                s = lax.dot_general(k_ref[0, c0:c0 + kc, :], q2, _NT, preferred_element_type=F32)
                e = jnp.exp2(s)
                p8 = e.reshape(kc // 8, 8, e.shape[1]).sum(axis=0)
                part = p8 if part is None else part + p8
                o = jnp.dot(vt_ref[0, :, c0:c0 + kc], e.astype(BF16), preferred_element_type=F32)
                acc = o if acc is None else acc + o
        den = part.sum(axis=0, keepdims=True)
    _attn_finish(acc, den, lam_ref[0], g_ref, o_ref)


def _attn(lam, q, segs, g_sub, tq, shift):
    nb, n, bw = q.shape
    in_specs = [
        pl.BlockSpec(memory_space=pltpu.SMEM),
        pl.BlockSpec((1, tq, B_V_DIM), lambda b, h, i: (b, i, h)),
    ]
    args = [lam, q]
    for (k, vt) in segs:
        lk = k.shape[1]
        in_specs += [pl.BlockSpec((1, lk, B_V_DIM), lambda b, h, i: (b, 0, h)),
                     pl.BlockSpec((1, B_V_DIM, lk), lambda b, h, i: (b, h, 0))]
        args += [k, vt]
    in_specs.append(pl.BlockSpec((1, B_V_DIM), lambda b, h, i: (0, 0)))
    args.append(g_sub)
    return pl.pallas_call(
        functools.partial(_attn_kernel, nseg=len(segs), shift=shift),
        grid=(nb, B_HEADS, n // tq),
        in_specs=in_specs,
        out_specs=pl.BlockSpec((1, tq, B_V_DIM), lambda b, h, i: (b, i, h)),
        out_shape=jax.ShapeDtypeStruct((nb, n, bw), BF16),
        compiler_params=_cparams(("parallel", "parallel", "parallel")),
        name="attn%d%s" % (len(segs), "_shift" if shift else ""),
    )(*args)


def _attn_guarded(score_bound, lam, q, segs, g_sub, tq, tq_shift):
    flat = [a for kv in segs for a in kv]

    def run(shift, lam, q, g_sub, *flat):
        segs = [(flat[2 * s], flat[2 * s + 1]) for s in range(len(flat) // 2)]
        return _attn(lam, q, segs, g_sub, tq_shift if shift else tq, shift)

    return lax.cond(score_bound <= ATTN_UNSHIFTED_MAX_SCORE,
                    functools.partial(run, False), functools.partial(run, True), lam, q, g_sub, *flat)


def _merge_kernel(x_ref, mod_ref, ya_ref, yb_ref, glu_ref, glup_ref, glun_ref, cw_ref, cb_ref,
                  lcg_ref, lcb_ref, wo_ref, o_ref, ext_ref, sh_ref, yc_ref, *, D):
    i = pl.program_id(1)
    nt = pl.num_programs(1)
    tm = x_ref.shape[1]
    prev = glup_ref[0].astype(F32)
    nxt = glun_ref[0].astype(F32)
    ext_ref[0:HALO, :] = jnp.where(i > 0, prev, jnp.zeros_like(prev))
    ext_ref[HALO:HALO + tm, :] = glu_ref[0].astype(F32)
    ext_ref[HALO + tm:, :] = jnp.where(i < nt - 1, nxt, jnp.zeros_like(nxt))
    base = HALO - C_CONV // 2
    srcs = {}
    for r in range(8):
        off = base + r
        n_taps = len(range(r, C_CONV, 8))
        if off % 8 == 0:
            srcs[r] = (ext_ref, off)
        else:
            span = tm + 8 * (n_taps - 1)
            sh_ref[r, 0:span, :] = ext_ref[off:off + span, :]
            srcs[r] = (sh_ref.at[r], 0)
    for rb in range(0, tm, CONV_ROW_BLOCK):
        acc = jnp.broadcast_to(cb_ref[...], (CONV_ROW_BLOCK, cb_ref.shape[1]))
        for j in range(C_CONV):
            src, off = srcs[j % 8]
            r0 = off + 8 * (j // 8) + rb
            acc = acc + cw_ref[j:j + 1, :] * src[r0:r0 + CONV_ROW_BLOCK, :]
        yc_ref[rb:rb + CONV_ROW_BLOCK, :] = acc
    yc = _silu(_ln(yc_ref[...], lcg_ref[...], lcb_ref[...]))
    mix = jnp.concatenate([ya_ref[0], yb_ref[0], yc.astype(BF16)], axis=-1)
    out = jnp.dot(mix, wo_ref[...], preferred_element_type=F32)
    gate = mod_ref[0][:, 2 * D:3 * D]
    o_ref[0] = x_ref[0] + gate * out


def _merge(x, mod, ya, yb, glu, cw, cb, lcg, lcb, w_out, tm):
    nb, n, D = x.shape
    aw, bw = D // 4, D // 2
    r = tm // HALO
    nh = n // HALO
    const = lambda b, i: (0, 0)
    tok = lambda b, i: (b, i, 0)
    return pl.pallas_call(
        functools.partial(_merge_kernel, D=D),
        grid=(nb, n // tm),
        in_specs=[
            pl.BlockSpec((1, tm, D), tok),
            pl.BlockSpec((1, 1, 6 * D), lambda b, i: (b, 0, 0)),
            pl.BlockSpec((1, tm, aw), tok),
            pl.BlockSpec((1, tm, bw), tok),
            pl.BlockSpec((1, tm, aw), tok),
            pl.BlockSpec((1, HALO, aw), lambda b, i: (b, jnp.maximum(i * r - 1, 0), 0)),
            pl.BlockSpec((1, HALO, aw), lambda b, i: (b, jnp.minimum((i + 1) * r, nh - 1), 0)),
            pl.BlockSpec((C_CONV, aw), const),
            pl.BlockSpec((1, aw), const),
            pl.BlockSpec((1, aw), const),
            pl.BlockSpec((1, aw), const),
            pl.BlockSpec((D, D), const),
        ],
        out_specs=pl.BlockSpec((1, tm, D), tok),
        out_shape=jax.ShapeDtypeStruct((nb, n, D), F32),
        scratch_shapes=[pltpu.VMEM((tm + 2 * HALO, aw), F32),
                        pltpu.VMEM((8, tm + 8 * (C_CONV // 8), aw), F32),
                        pltpu.VMEM((tm, aw), F32)],
        compiler_params=_cparams(("parallel", "parallel")),
        name="merge",
    )(x, mod, ya, yb, glu, glu, glu, cw, cb, lcg, lcb, w_out)


def _ffn_chunks(F):
    step = 4 * V7X_MXU_DIM
    return [(c, min(c + step, F)) for c in range(0, F, step)]


def _ffn_kernel(x_ref, xp_ref, xn_ref, mod_ref, g2_ref, wg_ref, wv_ref, wd_ref, cw_ref, cb_ref,
                o_ref, gt_ref, *, D):
    i = pl.program_id(1)
    nt = pl.num_programs(1)
    tm = x_ref.shape[1]
    F = wg_ref.shape[1]
    mod = mod_ref[0]
    shift, scale, gate = mod[:, 3 * D:4 * D], mod[:, 4 * D:5 * D], mod[:, 5 * D:6 * D]

    def h2(xr):
        return (_rms(xr, g2_ref[...]) * (1.0 + scale) + shift).astype(BF16)

    x = x_ref[0]
    hp = h2(xp_ref[0])
    hn = h2(xn_ref[0])
    hp = jnp.where(i > 0, hp, jnp.zeros_like(hp))
    hn = jnp.where(i < nt - 1, hn, jnp.zeros_like(hn))
    hm = h2(x)
    hext = jnp.concatenate([hp, hm, hn], axis=0)
    chunks = _ffn_chunks(F)

    def up(c0, c1):
        gt_ref[:, c0:c1] = jnp.dot(hext, wg_ref[:, c0:c1], preferred_element_type=F32)
        return jnp.dot(hm, wv_ref[:, c0:c1], preferred_element_type=F32)

    def act(c0, c1, val):
        cw = cw_ref[:, c0:c1]
        conv = (cw[0:1] * gt_ref[HALO - 1:HALO - 1 + tm, c0:c1] + cw[1:2] * gt_ref[HALO:HALO + tm, c0:c1]
                + cw[2:3] * gt_ref[HALO + 1:HALO + 1 + tm, c0:c1] + cb_ref[:, c0:c1])
        return (_silu(conv) * val).astype(BF16)

    acc = None
    val = up(*chunks[0])
    for ci, (c0, c1) in enumerate(chunks):
        val_next = up(*chunks[ci + 1]) if ci + 1 < len(chunks) else None
        o = jnp.dot(act(c0, c1, val), wd_ref[c0:c1, :], preferred_element_type=F32)
        acc = o if acc is None else acc + o
        val = val_next
    o_ref[0] = x + gate * acc


def _ffn(x, mod, g2, wg, wv, wd, cw, cb, tm):
    nb, n, D = x.shape
    F = wg.shape[1]
    r = tm // HALO
    nh = n // HALO
    const = lambda b, i: (0, 0)
    tok = lambda b, i: (b, i, 0)
    resident = dict(pipeline_mode=pl.Buffered(1))
    return pl.pallas_call(
        functools.partial(_ffn_kernel, D=D),
        grid=(nb, n // tm),
        in_specs=[
            pl.BlockSpec((1, tm, D), tok),
            pl.BlockSpec((1, HALO, D), lambda b, i: (b, jnp.maximum(i * r - 1, 0), 0)),
            pl.BlockSpec((1, HALO, D), lambda b, i: (b, jnp.minimum((i + 1) * r, nh - 1), 0)),
            pl.BlockSpec((1, 1, 6 * D), lambda b, i: (b, 0, 0)),
            pl.BlockSpec((1, D), const),
            pl.BlockSpec((D, F), const, **resident),
            pl.BlockSpec((D, F), const, **resident),
            pl.BlockSpec((F, D), const, **resident),
            pl.BlockSpec((FFN_CONV, F), const),
            pl.BlockSpec((1, F), const),
        ],
        out_specs=pl.BlockSpec((1, tm, D), tok),
        out_shape=jax.ShapeDtypeStruct((nb, n, D), F32),
        scratch_shapes=[pltpu.VMEM((tm + 2 * HALO, F), F32)],
        compiler_params=_cparams(("parallel", "parallel")),
        name="ffn",
    )(x, x, x, mod, g2, wg, wv, wd, cw, cb)


def _rope_tables(n):
    t = jnp.arange(n)
    pos = jnp.stack([(t // GRID_W).astype(F32), (t % GRID_W).astype(F32)], axis=1)
    half_dim = B_QK_DIM // 2
    inv = ROPE_THETA ** (-jnp.arange(0, half_dim, 2, dtype=F32) / half_dim)
    lane = jnp.arange(128)
    g64 = lane % B_QK_DIM
    axis = g64 // half_dim
    second = (g64 % half_dim) // (half_dim // 2)
    ang = pos[:, axis] * inv[g64 % (half_dim // 2)][None, :]
    cos, sin = jnp.cos(ang), jnp.sin(ang)
    zero = jnp.zeros_like(sin)
    return cos, jnp.where(second[None, :] == 0, -sin, zero), jnp.where(second[None, :] == 1, sin, zero)


def kernel(x, c, ctx, c_ctx, w_mod, b_mod, norm1_g, w_in, ln_v_g, ln_v_b, w_s, b_s, q_norm_g, k_norm_g,
           lam_q1, lam_k1, lam_q2, lam_k2, subln_g, conv_w, conv_b, ln_c_g, ln_c_b, w_out, norm2_g,
           w_gate, w_val, ffn_conv_w, ffn_conv_b, w_down):
    nb, n, D = x.shape
    n_ctx = ctx.shape[1]
    L = w_mod.shape[0]
    aw = D // 4
    assert nb + 1 <= MOD_ROWS and n % GRID_W == 0

    cvec = jnp.zeros((MOD_ROWS, D), F32).at[:nb].set(c).at[nb].set(c_ctx)
    mods, lam_dyn = _adaln(cvec, w_mod, b_mod, lam_q1, lam_k1, lam_q2, lam_k2)

    tm = min(1024, n)
    tm_c = min(1024, n_ctx)
    tq = min(2048, n)
    tq_c = min(256, n_ctx)
    rope_tabs = _rope_tables(n)
    gsum = jnp.kron(jnp.eye(V7X_MXU_DIM // B_QK_DIM, dtype=F32), jnp.ones((B_QK_DIM, B_QK_DIM), F32)).astype(BF16)
    reps = (D // 2) // B_QK_DIM

    x_lat, x_ctx = x, ctx
    for l in range(L):
        last = l == L - 1
        lam_init = 0.8 - 0.6 * math.exp(-0.3 * l)
        lam = lam_dyn[l, 0, 0:1] + lam_init
        mod_l = mods[l, :nb].reshape(nb, 1, 6 * D)
        mod_c = jnp.broadcast_to(mods[l, nb].reshape(1, 1, 6 * D), (nb, 1, 6 * D))
        row = lambda a: a[l].reshape(1, -1)
        w_in_b = w_in[l].astype(BF16)
        ws_all = w_s[l].reshape(A_HEADS * CHUNK, CHUNK).astype(BF16)
        bs_full = jnp.repeat(b_s[l].T, aw // A_HEADS, axis=1)
        gq = jnp.tile(q_norm_g[l], reps).reshape(1, -1) * (B_QK_DIM ** -0.5 * math.log2(math.e))
        score_bound = 1.02 * B_QK_DIM ** 0.5 * jnp.max(jnp.abs(q_norm_g[l])) * jnp.max(jnp.abs(k_norm_g[l]))
        gk = jnp.tile(k_norm_g[l], reps).reshape(1, -1)
        g_sub = row(subln_g) * (1.0 - lam_init)
        proj_args = (row(norm1_g), w_in_b, row(ln_v_g), row(ln_v_b), ws_all, bs_full, gq, gk, gsum)
        merge_args = (conv_w[l], row(conv_b), row(ln_c_g), row(ln_c_b), w_out[l].astype(BF16))
        ffn_args = (row(norm2_g), w_gate[l].astype(BF16), w_val[l].astype(BF16), w_down[l].astype(BF16),
                    ffn_conv_w[l], row(ffn_conv_b))

        ya_c, q_c, k_c, vt_c, glu_c = _proj(x_ctx, mod_c, *proj_args, None, tm_c)
        if not last:
            yb_c = _attn(lam, q_c, [(k_c, vt_c)], g_sub, tq_c, True)
            x_ctx = _merge(x_ctx, mod_c, ya_c, yb_c, glu_c, *merge_args, tm_c)
            x_ctx = _ffn(x_ctx, mod_c, *ffn_args, tm_c)

        ya, q, k, vt, glu = _proj(x_lat, mod_l, *proj_args, rope_tabs, tm)
        yb = _attn_guarded(score_bound, lam, q, [(k, vt), (k_c, vt_c)], g_sub, tq, tq_c)
        x_lat = _merge(x_lat, mod_l, ya, yb, glu, *merge_args, tm)
        x_lat = _ffn(x_lat, mod_l, *ffn_args, tm)
    return x_lat
```

```python
import functools
import math

import jax
import jax.numpy as jnp
from jax import lax
from jax.experimental import pallas as pl
from jax.experimental.pallas import tpu as pltpu

F32 = jnp.float32
BF16 = jnp.bfloat16

EPS = 1e-6
GRID_W = 64
ROPE_THETA = 10000.0
A_HEADS = 4
CHUNK = 128
B_HEADS = 4
B_QK_DIM = 64
B_V_DIM = 128
C_CONV = 31
FFN_CONV = 3

V7X_MXU_DIM = 256
V7X_VMEM_LIMIT_BYTES = 56 * 1024 * 1024
BF16_SUBLANES = 16
HALO = 16
MOD_ROWS = 24
ATTN_KEY_CHUNK = 512
CONV_ROW_BLOCK = 64
CAST_ROW_BLOCK = 256
ATTN_UNSHIFTED_MAX_SCORE = 64.0


def _cparams(sem):
    return pltpu.CompilerParams(dimension_semantics=sem, vmem_limit_bytes=V7X_VMEM_LIMIT_BYTES)


def _rms(x, g):
    return x * lax.rsqrt(jnp.mean(x * x, axis=-1, keepdims=True) + EPS) * g


def _ln(x, g, b):
    xc = x - jnp.mean(x, axis=-1, keepdims=True)
    return xc * lax.rsqrt(jnp.mean(xc * xc, axis=-1, keepdims=True) + EPS) * g + b


def _silu(x):
    return x * jax.nn.sigmoid(x)


def _adaln_kernel(c_ref, w_ref, b_ref, q1_ref, k1_ref, q2_ref, k2_ref, o_ref, lam_ref):
    s = _silu(c_ref[...])
    o_ref[0] = jnp.dot(s.astype(BF16), w_ref[0].astype(BF16), preferred_element_type=F32) + b_ref[0]
    d1 = jnp.sum(q1_ref[0] * k1_ref[0], axis=-1, keepdims=True)
    d2 = jnp.sum(q2_ref[0] * k2_ref[0], axis=-1, keepdims=True)
    lam_ref[0] = jnp.broadcast_to(jnp.exp(d1) - jnp.exp(d2), lam_ref.shape[1:])


def _adaln(cvec, w_mod, b_mod, lq1, lk1, lq2, lk2):
    L, D, N6 = w_mod.shape
    tn = N6 // 4
    small = pl.BlockSpec((1, 1, B_QK_DIM), lambda l, j: (l, 0, 0))
    return pl.pallas_call(
        _adaln_kernel,
        grid=(L, N6 // tn),
        in_specs=[
            pl.BlockSpec((MOD_ROWS, D), lambda l, j: (0, 0)),
            pl.BlockSpec((1, D, tn), lambda l, j: (l, 0, j)),
            pl.BlockSpec((1, 1, tn), lambda l, j: (l, 0, j)),
            small, small, small, small,
        ],
        out_specs=[
            pl.BlockSpec((1, MOD_ROWS, tn), lambda l, j: (l, 0, j)),
            pl.BlockSpec((1, 8, 128), lambda l, j: (l, 0, 0)),
        ],
        out_shape=[
            jax.ShapeDtypeStruct((L, MOD_ROWS, N6), F32),
            jax.ShapeDtypeStruct((L, 8, 128), F32),
        ],
        compiler_params=_cparams(("arbitrary", "arbitrary")),
        name="adaln",
    )(cvec, w_mod, b_mod.reshape(L, 1, N6), lq1.reshape(L, 1, -1), lk1.reshape(L, 1, -1),
      lq2.reshape(L, 1, -1), lk2.reshape(L, 1, -1))


def _cast_kernel(w_ref, o_ref):
    o_ref[...] = w_ref[...].astype(BF16)


def _to_bf16(w):
    L, R, C = w.shape
    tr = min(CAST_ROW_BLOCK, R)
    spec = pl.BlockSpec((1, tr, C), lambda l, i: (l, i, 0))
    return pl.pallas_call(
        _cast_kernel,
        grid=(L, R // tr),
        in_specs=[spec],
        out_specs=spec,
        out_shape=jax.ShapeDtypeStruct(w.shape, BF16),
        compiler_params=_cparams(("parallel", "parallel")),
        name="cast_bf16",
    )(w)


def _proj_kernel(*refs, rope, kv_only, D):
    (x_ref, mod_ref, g1_ref, w_ref, lng_ref, lnb_ref, ws_ref, bs_ref, gq_ref, gk_ref, gsum_ref) = refs[:11]
    if rope:
        cos_ref, sa_ref, sb_ref = refs[11:14]
        outs = refs[14:]
    else:
        outs = refs[11:]
    if kv_only:
        k_ref, vt_ref = outs
    else:
        ya_ref, q_ref, k_ref, vt_ref, glu_ref = outs

    aw = D // 4
    bw = D // 2
    x = x_ref[0]
    tm = x.shape[0]
    mod = mod_ref[0]
    shift, scale = mod[:, 0:D], mod[:, D:2 * D]
    h = (_rms(x, g1_ref[...]) * (1.0 + scale) + shift).astype(BF16)
    c_q = 2 * aw
    c_k = c_q + bw
    c_v = c_k + bw
    c_c = c_v + bw

    def cols(c0, c1):
        return jnp.dot(h, w_ref[:, c0:c1], preferred_element_type=F32)

    def epilogue_a(pa):
        z = jax.nn.gelu(pa)
        u, v = z[:, :aw], z[:, aw:]
        vn = _ln(v, lng_ref[...], lnb_ref[...]).astype(BF16)
        head = lax.broadcasted_iota(jnp.int32, (CHUNK, aw), 1) // (aw // A_HEADS)
        ws = ws_ref[...]
        for c in range(tm // CHUNK):
            r = jnp.dot(ws, vn[c * CHUNK:(c + 1) * CHUNK, :], preferred_element_type=F32)
            gate = r[0:CHUNK]
            for hh in range(1, A_HEADS):
                gate = jnp.where(head == hh, r[hh * CHUNK:(hh + 1) * CHUNK], gate)
            gate = gate + bs_ref[...]
            ya_ref[0, c * CHUNK:(c + 1) * CHUNK, :] = (u[c * CHUNK:(c + 1) * CHUNK] * gate).astype(BF16)

    def qk_norm(t, g):
        sq = (t * t).astype(BF16)
        parts = [jnp.dot(sq[:, j:j + V7X_MXU_DIM], gsum_ref[...], preferred_element_type=F32)
                 for j in range(0, bw, V7X_MXU_DIM)]
        ss = jnp.concatenate(parts, axis=-1)
        t = t * lax.rsqrt(ss * (1.0 / B_QK_DIM) + EPS) * g
        if not rope:
            return t
        cos, sa, sb = cos_ref[...], sa_ref[...], sb_ref[...]
        blocks = []
        for j in range(0, bw, 128):
            tb = t[:, j:j + 128]
            blocks.append(tb * cos + pltpu.roll(tb, 128 - 16, 1) * sa + pltpu.roll(tb, 16, 1) * sb)
        return jnp.concatenate(blocks, axis=-1)

    if kv_only:
        pk = cols(c_k, c_v)
        pv = cols(c_v, c_c)
        k_ref[0] = qk_norm(pk, gk_ref[...]).astype(BF16)
        vt_ref[0] = pv.astype(BF16).T
        return
    pa = cols(0, c_q)
    pq = cols(c_q, c_k)
    epilogue_a(pa)
    pk = cols(c_k, c_v)
    q_ref[0] = qk_norm(pq, gq_ref[...]).astype(BF16)
    pv = cols(c_v, c_c)
    k_ref[0] = qk_norm(pk, gk_ref[...]).astype(BF16)
    pc = cols(c_c, c_c + 2 * aw)
    vt_ref[0] = pv.astype(BF16).T
    glu_ref[0] = (pc[:, :aw] * jax.nn.sigmoid(pc[:, aw:])).astype(BF16)


def _proj(layer, x, mod, g1, w_in, lng, lnb, ws_all, bs_full, gq, gk, gsum, rope_tabs, tm, kv_only=False):
    nb, n, D = x.shape
    aw, bw = D // 4, D // 2
    cols = w_in.shape[2]
    of_layer = lambda b, i: (layer, 0, 0)
    rope = rope_tabs is not None
    const = lambda b, i: (0, 0)
    tok = lambda b, i: (b, i, 0)
    in_specs = [
        pl.BlockSpec((1, tm, D), tok),
        pl.BlockSpec((1, 1, 6 * D), lambda b, i: (b, 0, 0)),
        pl.BlockSpec((1, D), const),
        pl.BlockSpec((None, D, cols), of_layer),
        pl.BlockSpec((1, aw), const),
        pl.BlockSpec((1, aw), const),
        pl.BlockSpec((None, A_HEADS * CHUNK, CHUNK), of_layer),
        pl.BlockSpec((CHUNK, aw), const),
        pl.BlockSpec((1, bw), const),
        pl.BlockSpec((1, bw), const),
        pl.BlockSpec((V7X_MXU_DIM, V7X_MXU_DIM), const),
    ]
    args = [x, mod, g1, w_in, lng, lnb, ws_all, bs_full, gq, gk, gsum]
    if rope:
        in_specs += [pl.BlockSpec((tm, 128), lambda b, i: (i, 0))] * 3
        args += list(rope_tabs)
    tok_spec = lambda w: pl.BlockSpec((1, tm, w), tok)
    tok_shape = lambda w: jax.ShapeDtypeStruct((nb, n, w), BF16)
    out_specs = [tok_spec(aw), tok_spec(bw), tok_spec(bw),
                 pl.BlockSpec((1, bw, tm), lambda b, i: (b, 0, i)), tok_spec(aw)]
    out_shape = [tok_shape(aw), tok_shape(bw), tok_shape(bw),
                 jax.ShapeDtypeStruct((nb, bw, n), BF16), tok_shape(aw)]
    if kv_only:
        out_specs, out_shape = out_specs[2:4], out_shape[2:4]
    return pl.pallas_call(
        functools.partial(_proj_kernel, rope=rope, kv_only=kv_only, D=D),
        grid=(nb, n // tm),
        in_specs=in_specs,
        out_specs=out_specs,
        out_shape=out_shape,
        compiler_params=_cparams(("parallel", "parallel")),
        name=("proj_rope" if rope else "proj") + ("_kv" if kv_only else ""),
    )(*args)


def _stack_maps(q):
    lane = lax.broadcasted_iota(jnp.int32, q.shape, 1)
    zero = jnp.zeros_like(q)
    return jnp.concatenate([jnp.where(lane < B_QK_DIM, q, zero), jnp.where(lane >= B_QK_DIM, q, zero)], axis=0)


def _attn_finish(acc_t, den, lam, g_ref, o_ref):
    tq = acc_t.shape[1] // 2
    o2 = acc_t / den
    o = o2[:, :tq] - lam * o2[:, tq:]
    z = o * lax.rsqrt(jnp.mean(o * o, axis=0, keepdims=True) + EPS)
    o_ref[0] = (z.T * g_ref[...]).astype(BF16)


_NT = (((1,), (1,)), ((), ()))


def _attn_kernel(lam_ref, q_ref, *refs, nseg, shift):
    kv = refs[:2 * nseg]
    g_ref, o_ref = refs[2 * nseg], refs[2 * nseg + 1]
    q2 = _stack_maps(q_ref[0])
    if shift:
        ss = [lax.dot_general(kv[2 * s][0], q2, _NT, preferred_element_type=F32) for s in range(nseg)]
        m = ss[0].max(axis=0, keepdims=True)
        for s in ss[1:]:
            m = jnp.maximum(m, s.max(axis=0, keepdims=True))
        acc = None
        den = None
        for si, s in enumerate(ss):
            e = jnp.exp2(s - m)
            d = e.sum(axis=0, keepdims=True)
            o = jnp.dot(kv[2 * si + 1][0], e.astype(BF16), preferred_element_type=F32)
            acc = o if acc is None else acc + o
            den = d if den is None else den + d
    else:
        acc = None
        part = None
        for si in range(nseg):
            k_ref, vt_ref = kv[2 * si], kv[2 * si + 1]
            lk = k_ref.shape[1]
            kc = min(ATTN_KEY_CHUNK, lk)
            for c0 in range(0, lk, kc):
                s = lax.dot_general(k_ref[0, c0:c0 + kc, :], q2, _NT, preferred_element_type=F32)
                e = jnp.exp2(s)
                p8 = e.reshape(kc // 8, 8, e.shape[1]).sum(axis=0)
                part = p8 if part is None else part + p8
                o = jnp.dot(vt_ref[0, :, c0:c0 + kc], e.astype(BF16), preferred_element_type=F32)
                acc = o if acc is None else acc + o
        den = part.sum(axis=0, keepdims=True)
    _attn_finish(acc, den, lam_ref[0], g_ref, o_ref)


def _attn(lam, q, segs, g_sub, tq, shift):
    nb, n, bw = q.shape
    in_specs = [
        pl.BlockSpec(memory_space=pltpu.SMEM),
        pl.BlockSpec((1, tq, B_V_DIM), lambda b, h, i: (b, i, h)),
    ]
    args = [lam, q]
    for (k, vt) in segs:
        lk = k.shape[1]
        in_specs += [pl.BlockSpec((1, lk, B_V_DIM), lambda b, h, i: (b, 0, h)),
                     pl.BlockSpec((1, B_V_DIM, lk), lambda b, h, i: (b, h, 0))]
        args += [k, vt]
    in_specs.append(pl.BlockSpec((1, B_V_DIM), lambda b, h, i: (0, 0)))
    args.append(g_sub)
    return pl.pallas_call(
        functools.partial(_attn_kernel, nseg=len(segs), shift=shift),
        grid=(nb, B_HEADS, n // tq),
        in_specs=in_specs,
        out_specs=pl.BlockSpec((1, tq, B_V_DIM), lambda b, h, i: (b, i, h)),
        out_shape=jax.ShapeDtypeStruct((nb, n, bw), BF16),
        compiler_params=_cparams(("parallel", "parallel", "parallel")),
        name="attn%d%s" % (len(segs), "_shift" if shift else ""),
    )(*args)


def _attn_guarded(score_bound, lam, q, segs, g_sub, tq, tq_shift):
    flat = [a for kv in segs for a in kv]

    def run(shift, lam, q, g_sub, *flat):
        segs = [(flat[2 * s], flat[2 * s + 1]) for s in range(len(flat) // 2)]
        return _attn(lam, q, segs, g_sub, tq_shift if shift else tq, shift)

    return lax.cond(score_bound <= ATTN_UNSHIFTED_MAX_SCORE,
                    functools.partial(run, False), functools.partial(run, True), lam, q, g_sub, *flat)


def _merge_kernel(x_ref, mod_ref, ya_ref, yb_ref, glu_ref, glup_ref, glun_ref, cw_ref, cb_ref,
                  lcg_ref, lcb_ref, wo_ref, o_ref, ext_ref, sh_ref, yc_ref, *, D):
    i = pl.program_id(1)
    nt = pl.num_programs(1)
    tm = x_ref.shape[1]
    prev = glup_ref[0].astype(F32)
    nxt = glun_ref[0].astype(F32)
    ext_ref[0:HALO, :] = jnp.where(i > 0, prev, jnp.zeros_like(prev))
    ext_ref[HALO:HALO + tm, :] = glu_ref[0].astype(F32)
    ext_ref[HALO + tm:, :] = jnp.where(i < nt - 1, nxt, jnp.zeros_like(nxt))
    base = HALO - C_CONV // 2
    srcs = {}
    for r in range(8):
        off = base + r
        n_taps = len(range(r, C_CONV, 8))
        if off % 8 == 0:
            srcs[r] = (ext_ref, off)
        else:
            span = tm + 8 * (n_taps - 1)
            sh_ref[r, 0:span, :] = ext_ref[off:off + span, :]
            srcs[r] = (sh_ref.at[r], 0)
    for rb in range(0, tm, CONV_ROW_BLOCK):
        acc = jnp.broadcast_to(cb_ref[...], (CONV_ROW_BLOCK, cb_ref.shape[1]))
        for j in range(C_CONV):
            src, off = srcs[j % 8]
            r0 = off + 8 * (j // 8) + rb
            acc = acc + cw_ref[j:j + 1, :] * src[r0:r0 + CONV_ROW_BLOCK, :]
        yc_ref[rb:rb + CONV_ROW_BLOCK, :] = acc
    yc = _silu(_ln(yc_ref[...], lcg_ref[...], lcb_ref[...]))
    mix = jnp.concatenate([ya_ref[0], yb_ref[0], yc.astype(BF16)], axis=-1)
    out = jnp.dot(mix, wo_ref[...], preferred_element_type=F32)
    gate = mod_ref[0][:, 2 * D:3 * D]
    o_ref[0] = x_ref[0] + gate * out


def _merge(layer, x, mod, ya, yb, glu, cw, cb, lcg, lcb, w_out, tm):
    nb, n, D = x.shape
    aw, bw = D // 4, D // 2
    r = tm // HALO
    nh = n // HALO
    const = lambda b, i: (0, 0)
    tok = lambda b, i: (b, i, 0)
    return pl.pallas_call(
        functools.partial(_merge_kernel, D=D),
        grid=(nb, n // tm),
        in_specs=[
            pl.BlockSpec((1, tm, D), tok),
            pl.BlockSpec((1, 1, 6 * D), lambda b, i: (b, 0, 0)),
            pl.BlockSpec((1, tm, aw), tok),
            pl.BlockSpec((1, tm, bw), tok),
            pl.BlockSpec((1, tm, aw), tok),
            pl.BlockSpec((1, HALO, aw), lambda b, i: (b, jnp.maximum(i * r - 1, 0), 0)),
            pl.BlockSpec((1, HALO, aw), lambda b, i: (b, jnp.minimum((i + 1) * r, nh - 1), 0)),
            pl.BlockSpec((C_CONV, aw), const),
            pl.BlockSpec((1, aw), const),
            pl.BlockSpec((1, aw), const),
            pl.BlockSpec((1, aw), const),
            pl.BlockSpec((None, D, D), lambda b, i: (layer, 0, 0)),
        ],
        out_specs=pl.BlockSpec((1, tm, D), tok),
        out_shape=jax.ShapeDtypeStruct((nb, n, D), F32),
        scratch_shapes=[pltpu.VMEM((tm + 2 * HALO, aw), F32),
                        pltpu.VMEM((8, tm + 8 * (C_CONV // 8), aw), F32),
                        pltpu.VMEM((tm, aw), F32)],
        compiler_params=_cparams(("parallel", "parallel")),
        name="merge",
    )(x, mod, ya, yb, glu, glu, glu, cw, cb, lcg, lcb, w_out)


def _ffn_kernel(x_ref, xp_ref, xn_ref, mod_ref, g2_ref, wg_ref, wv_ref, wd_ref, cw_ref, cb_ref,
                o_ref, gt_ref, *, D):
    i = pl.program_id(1)
    nt = pl.num_programs(1)
    tm = x_ref.shape[1]
    mod = mod_ref[0]
    shift, scale, gate = mod[:, 3 * D:4 * D], mod[:, 4 * D:5 * D], mod[:, 5 * D:6 * D]

    def h2(xr):
        return (_rms(xr, g2_ref[...]) * (1.0 + scale) + shift).astype(BF16)

    x = x_ref[0]
    hp = h2(xp_ref[0])
    hn = h2(xn_ref[0])
    hp = jnp.where(i > 0, hp, jnp.zeros_like(hp))
    hn = jnp.where(i < nt - 1, hn, jnp.zeros_like(hn))
    hm = h2(x)
    hext = jnp.concatenate([hp, hm, hn], axis=0)
    gt_ref[...] = jnp.dot(hext, wg_ref[...], preferred_element_type=F32)
    val = jnp.dot(hm, wv_ref[...], preferred_element_type=F32)
    cw = cw_ref[...]
    conv = (cw[0:1] * gt_ref[HALO - 1:HALO - 1 + tm, :] + cw[1:2] * gt_ref[HALO:HALO + tm, :]
            + cw[2:3] * gt_ref[HALO + 1:HALO + 1 + tm, :] + cb_ref[...])
    act = (_silu(conv) * val).astype(BF16)
    o_ref[0] = x + gate * jnp.dot(act, wd_ref[...], preferred_element_type=F32)


def _ffn(layer, x, mod, g2, wg, wv, wd, cw, cb, tm):
    nb, n, D = x.shape
    F = wg.shape[2]
    of_layer = lambda b, i: (layer, 0, 0)
    r = tm // HALO
    nh = n // HALO
    const = lambda b, i: (0, 0)
    tok = lambda b, i: (b, i, 0)
    resident = dict(pipeline_mode=pl.Buffered(1))
    return pl.pallas_call(
        functools.partial(_ffn_kernel, D=D),
        grid=(nb, n // tm),
        in_specs=[
            pl.BlockSpec((1, tm, D), tok),
            pl.BlockSpec((1, HALO, D), lambda b, i: (b, jnp.maximum(i * r - 1, 0), 0)),
            pl.BlockSpec((1, HALO, D), lambda b, i: (b, jnp.minimum((i + 1) * r, nh - 1), 0)),
            pl.BlockSpec((1, 1, 6 * D), lambda b, i: (b, 0, 0)),
            pl.BlockSpec((1, D), const),
            pl.BlockSpec((None, D, F), of_layer, **resident),
            pl.BlockSpec((None, D, F), of_layer, **resident),
            pl.BlockSpec((None, F, D), of_layer, **resident),
            pl.BlockSpec((FFN_CONV, F), const),
            pl.BlockSpec((1, F), const),
        ],
        out_specs=pl.BlockSpec((1, tm, D), tok),
        out_shape=jax.ShapeDtypeStruct((nb, n, D), F32),
        scratch_shapes=[pltpu.VMEM((tm + 2 * HALO, F), F32)],
        compiler_params=_cparams(("parallel", "parallel")),
        name="ffn",
    )(x, x, x, mod, g2, wg, wv, wd, cw, cb)


def _rope_tables(n):
    t = jnp.arange(n)
    pos = jnp.stack([(t // GRID_W).astype(F32), (t % GRID_W).astype(F32)], axis=1)
    half_dim = B_QK_DIM // 2
    inv = ROPE_THETA ** (-jnp.arange(0, half_dim, 2, dtype=F32) / half_dim)
    lane = jnp.arange(128)
    g64 = lane % B_QK_DIM
    axis = g64 // half_dim
    second = (g64 % half_dim) // (half_dim // 2)
    ang = pos[:, axis] * inv[g64 % (half_dim // 2)][None, :]
    cos, sin = jnp.cos(ang), jnp.sin(ang)
    zero = jnp.zeros_like(sin)
    return cos, jnp.where(second[None, :] == 0, -sin, zero), jnp.where(second[None, :] == 1, sin, zero)


def kernel(x, c, ctx, c_ctx, w_mod, b_mod, norm1_g, w_in, ln_v_g, ln_v_b, w_s, b_s, q_norm_g, k_norm_g,
           lam_q1, lam_k1, lam_q2, lam_k2, subln_g, conv_w, conv_b, ln_c_g, ln_c_b, w_out, norm2_g,
           w_gate, w_val, ffn_conv_w, ffn_conv_b, w_down):
    nb, n, D = x.shape
    n_ctx = ctx.shape[1]
    L = w_mod.shape[0]
    aw = D // 4
    assert nb + 1 <= MOD_ROWS and n % GRID_W == 0

    cvec = jnp.zeros((MOD_ROWS, D), F32).at[:nb].set(c).at[nb].set(c_ctx)
    mods, lam_dyn = _adaln(cvec, w_mod, b_mod, lam_q1, lam_k1, lam_q2, lam_k2)

    tm = min(1024, n)
    tm_c = min(1024, n_ctx)
    tq = min(2048, n)
    tq_c = min(256, n_ctx)
    rope_tabs = _rope_tables(n)
    gsum = jnp.kron(jnp.eye(V7X_MXU_DIM // B_QK_DIM, dtype=F32), jnp.ones((B_QK_DIM, B_QK_DIM), F32)).astype(BF16)
    reps = (D // 2) // B_QK_DIM

    w_in_b, w_out_b, w_gate_b, w_val_b, w_down_b = (_to_bf16(w) for w in (w_in, w_out, w_gate, w_val, w_down))
    ws_b = _to_bf16(w_s.reshape(L, A_HEADS * CHUNK, CHUNK))

    x_lat, x_ctx = x, ctx
    for l in range(L):
        last = l == L - 1
        lam_init = 0.8 - 0.6 * math.exp(-0.3 * l)
        lam = lam_dyn[l, 0, 0:1] + lam_init
        mod_l = mods[l, :nb].reshape(nb, 1, 6 * D)
        mod_c = jnp.broadcast_to(mods[l, nb].reshape(1, 1, 6 * D), (nb, 1, 6 * D))
        row = lambda a: a[l].reshape(1, -1)
        bs_full = jnp.repeat(b_s[l].T, aw // A_HEADS, axis=1)
        gq = jnp.tile(q_norm_g[l], reps).reshape(1, -1) * (B_QK_DIM ** -0.5 * math.log2(math.e))
        score_bound = 1.02 * B_QK_DIM ** 0.5 * jnp.max(jnp.abs(q_norm_g[l])) * jnp.max(jnp.abs(k_norm_g[l]))
        gk = jnp.tile(k_norm_g[l], reps).reshape(1, -1)
        g_sub = row(subln_g) * (1.0 - lam_init)
        proj_args = (row(norm1_g), w_in_b, row(ln_v_g), row(ln_v_b), ws_b, bs_full, gq, gk, gsum)
        merge_args = (conv_w[l], row(conv_b), row(ln_c_g), row(ln_c_b), w_out_b)
        ffn_args = (row(norm2_g), w_gate_b, w_val_b, w_down_b, ffn_conv_w[l], row(ffn_conv_b))

        if last:
            k_c, vt_c = _proj(l, x_ctx, mod_c, *proj_args, None, tm_c, kv_only=True)
        else:
            ya_c, q_c, k_c, vt_c, glu_c = _proj(l, x_ctx, mod_c, *proj_args, None, tm_c)
            yb_c = _attn(lam, q_c, [(k_c, vt_c)], g_sub, tq_c, True)
            x_ctx = _merge(l, x_ctx, mod_c, ya_c, yb_c, glu_c, *merge_args, tm_c)
            x_ctx = _ffn(l, x_ctx, mod_c, *ffn_args, tm_c)

        ya, q, k, vt, glu = _proj(l, x_lat, mod_l, *proj_args, rope_tabs, tm)
        yb = _attn_guarded(score_bound, lam, q, [(k, vt), (k_c, vt_c)], g_sub, tq, tq_c)
        x_lat = _merge(l, x_lat, mod_l, ya, yb, glu, *merge_args, tm)
        x_lat = _ffn(l, x_lat, mod_l, *ffn_args, tm)
    return x_lat
```

```python
import functools
import math

import jax
import jax.numpy as jnp
from jax import lax
from jax.experimental import pallas as pl
from jax.experimental.pallas import tpu as pltpu

F32 = jnp.float32
BF16 = jnp.bfloat16

EPS = 1e-6
GRID_W = 64
ROPE_THETA = 10000.0
A_HEADS = 4
CHUNK = 128
B_HEADS = 4
B_QK_DIM = 64
B_V_DIM = 128
C_CONV = 31
FFN_CONV = 3

V7X_MXU_DIM = 256
V7X_VMEM_LIMIT_BYTES = 56 * 1024 * 1024
BF16_SUBLANES = 16
HALO = 16
MOD_ROWS = 24
ATTN_KEY_CHUNK = 512
CAST_ROW_BLOCK = 256
ATTN_UNSHIFTED_MAX_SCORE = 64.0


def _cparams(sem):
    return pltpu.CompilerParams(dimension_semantics=sem, vmem_limit_bytes=V7X_VMEM_LIMIT_BYTES)


def _rms(x, g):
    return x * lax.rsqrt(jnp.mean(x * x, axis=-1, keepdims=True) + EPS) * g


def _ln(x, g, b):
    xc = x - jnp.mean(x, axis=-1, keepdims=True)
    return xc * lax.rsqrt(jnp.mean(xc * xc, axis=-1, keepdims=True) + EPS) * g + b


def _silu(x):
    return x * jax.nn.sigmoid(x)


def _adaln_kernel(c_ref, w_ref, b_ref, q1_ref, k1_ref, q2_ref, k2_ref, o_ref, lam_ref):
    s = _silu(c_ref[...])
    o_ref[0] = jnp.dot(s.astype(BF16), w_ref[0].astype(BF16), preferred_element_type=F32) + b_ref[0]
    d1 = jnp.sum(q1_ref[0] * k1_ref[0], axis=-1, keepdims=True)
    d2 = jnp.sum(q2_ref[0] * k2_ref[0], axis=-1, keepdims=True)
    lam_ref[0] = jnp.broadcast_to(jnp.exp(d1) - jnp.exp(d2), lam_ref.shape[1:])


def _adaln(cvec, w_mod, b_mod, lq1, lk1, lq2, lk2):
    L, D, N6 = w_mod.shape
    tn = N6 // 4
    small = pl.BlockSpec((1, 1, B_QK_DIM), lambda l, j: (l, 0, 0))
    return pl.pallas_call(
        _adaln_kernel,
        grid=(L, N6 // tn),
        in_specs=[
            pl.BlockSpec((MOD_ROWS, D), lambda l, j: (0, 0)),
            pl.BlockSpec((1, D, tn), lambda l, j: (l, 0, j)),
            pl.BlockSpec((1, 1, tn), lambda l, j: (l, 0, j)),
            small, small, small, small,
        ],
        out_specs=[
            pl.BlockSpec((1, MOD_ROWS, tn), lambda l, j: (l, 0, j)),
            pl.BlockSpec((1, 8, 128), lambda l, j: (l, 0, 0)),
        ],
        out_shape=[
            jax.ShapeDtypeStruct((L, MOD_ROWS, N6), F32),
            jax.ShapeDtypeStruct((L, 8, 128), F32),
        ],
        compiler_params=_cparams(("arbitrary", "arbitrary")),
        name="adaln",
    )(cvec, w_mod, b_mod.reshape(L, 1, N6), lq1.reshape(L, 1, -1), lk1.reshape(L, 1, -1),
      lq2.reshape(L, 1, -1), lk2.reshape(L, 1, -1))


def _cast_kernel(w_ref, o_ref):
    o_ref[...] = w_ref[...].astype(BF16)


def _to_bf16(w):
    L, R, C = w.shape
    tr = min(CAST_ROW_BLOCK, R)
    spec = pl.BlockSpec((1, tr, C), lambda l, i: (l, i, 0))
    return pl.pallas_call(
        _cast_kernel,
        grid=(L, R // tr),
        in_specs=[spec],
        out_specs=spec,
        out_shape=jax.ShapeDtypeStruct(w.shape, BF16),
        compiler_params=_cparams(("parallel", "parallel")),
        name="cast_bf16",
    )(w)


def _proj_kernel(*refs, rope, kv_only, D):
    (x_ref, mod_ref, g1_ref, w_ref, lng_ref, lnb_ref, ws_ref, bs_ref, gq_ref, gk_ref, gsum_ref) = refs[:11]
    if rope:
        cos_ref, sa_ref, sb_ref = refs[11:14]
        outs = refs[14:]
    else:
        outs = refs[11:]
    if kv_only:
        k_ref, vt_ref = outs
    else:
        ya_ref, q_ref, k_ref, vt_ref, glu_ref = outs

    aw = D // 4
    bw = D // 2
    x = x_ref[0]
    tm = x.shape[0]
    mod = mod_ref[0]
    shift, scale = mod[:, 0:D], mod[:, D:2 * D]
    h = (_rms(x, g1_ref[...]) * (1.0 + scale) + shift).astype(BF16)
    c_q = 2 * aw
    c_k = c_q + bw
    c_v = c_k + bw
    c_c = c_v + bw

    def cols(c0, c1):
        return jnp.dot(h, w_ref[:, c0:c1], preferred_element_type=F32)

    def epilogue_a(pa):
        z = jax.nn.gelu(pa)
        u, v = z[:, :aw], z[:, aw:]
        vn = _ln(v, lng_ref[...], lnb_ref[...]).astype(BF16)
        head = lax.broadcasted_iota(jnp.int32, (CHUNK, aw), 1) // (aw // A_HEADS)
        ws = ws_ref[...]
        zero = jnp.zeros((CHUNK, aw), BF16)
        for c in range(tm // CHUNK):
            vc = vn[c * CHUNK:(c + 1) * CHUNK, :]
            vbd = jnp.concatenate([jnp.where(head == hh, vc, zero) for hh in range(A_HEADS)], axis=0)
            gate = jnp.dot(ws, vbd, preferred_element_type=F32) + bs_ref[...]
            ya_ref[0, c * CHUNK:(c + 1) * CHUNK, :] = (u[c * CHUNK:(c + 1) * CHUNK] * gate).astype(BF16)

    def qk_norm(t, g):
        sq = (t * t).astype(BF16)
        parts = [jnp.dot(sq[:, j:j + V7X_MXU_DIM], gsum_ref[...], preferred_element_type=F32)
                 for j in range(0, bw, V7X_MXU_DIM)]
        ss = jnp.concatenate(parts, axis=-1)
        t = t * lax.rsqrt(ss * (1.0 / B_QK_DIM) + EPS) * g
        if not rope:
            return t
        cos, sa, sb = cos_ref[...], sa_ref[...], sb_ref[...]
        blocks = []
        for j in range(0, bw, 128):
            tb = t[:, j:j + 128]
            blocks.append(tb * cos + pltpu.roll(tb, 128 - 16, 1) * sa + pltpu.roll(tb, 16, 1) * sb)
        return jnp.concatenate(blocks, axis=-1)

    if kv_only:
        pk = cols(c_k, c_v)
        pv = cols(c_v, c_c)
        k_ref[0] = qk_norm(pk, gk_ref[...]).astype(BF16)
        vt_ref[0] = pv.astype(BF16).T
        return
    pa = cols(0, c_q)
    pq = cols(c_q, c_k)
    epilogue_a(pa)
    pk = cols(c_k, c_v)
    q_ref[0] = qk_norm(pq, gq_ref[...]).astype(BF16)
    pv = cols(c_v, c_c)
    k_ref[0] = qk_norm(pk, gk_ref[...]).astype(BF16)
    pc = cols(c_c, c_c + 2 * aw)
    vt_ref[0] = pv.astype(BF16).T
    glu_ref[0] = (pc[:, :aw] * jax.nn.sigmoid(pc[:, aw:])).astype(BF16)


def _proj(layer, x, mod, g1, w_in, lng, lnb, ws_all, bs_full, gq, gk, gsum, rope_tabs, tm, kv_only=False):
    nb, n, D = x.shape
    aw, bw = D // 4, D // 2
    cols = w_in.shape[2]
    of_layer = lambda b, i: (layer, 0, 0)
    rope = rope_tabs is not None
    const = lambda b, i: (0, 0)
    tok = lambda b, i: (b, i, 0)
    in_specs = [
        pl.BlockSpec((1, tm, D), tok),
        pl.BlockSpec((1, 1, 6 * D), lambda b, i: (b, 0, 0)),
        pl.BlockSpec((1, D), const),
        pl.BlockSpec((None, D, cols), of_layer),
        pl.BlockSpec((1, aw), const),
        pl.BlockSpec((1, aw), const),
        pl.BlockSpec((None, CHUNK, A_HEADS * CHUNK), of_layer),
        pl.BlockSpec((CHUNK, aw), const),
        pl.BlockSpec((1, bw), const),
        pl.BlockSpec((1, bw), const),
        pl.BlockSpec((V7X_MXU_DIM, V7X_MXU_DIM), const),
    ]
    args = [x, mod, g1, w_in, lng, lnb, ws_all, bs_full, gq, gk, gsum]
    if rope:
        in_specs += [pl.BlockSpec((tm, 128), lambda b, i: (i, 0))] * 3
        args += list(rope_tabs)
    tok_spec = lambda w: pl.BlockSpec((1, tm, w), tok)
    tok_shape = lambda w: jax.ShapeDtypeStruct((nb, n, w), BF16)
    out_specs = [tok_spec(aw), tok_spec(bw), tok_spec(bw),
                 pl.BlockSpec((1, bw, tm), lambda b, i: (b, 0, i)), tok_spec(aw)]
    out_shape = [tok_shape(aw), tok_shape(bw), tok_shape(bw),
                 jax.ShapeDtypeStruct((nb, bw, n), BF16), tok_shape(aw)]
    if kv_only:
        out_specs, out_shape = out_specs[2:4], out_shape[2:4]
    return pl.pallas_call(
        functools.partial(_proj_kernel, rope=rope, kv_only=kv_only, D=D),
        grid=(nb, n // tm),
        in_specs=in_specs,
        out_specs=out_specs,
        out_shape=out_shape,
        compiler_params=_cparams(("parallel", "parallel")),
        name=("proj_rope" if rope else "proj") + ("_kv" if kv_only else ""),
    )(*args)


def _stack_maps(q):
    lane = lax.broadcasted_iota(jnp.int32, q.shape, 1)
    zero = jnp.zeros_like(q)
    return jnp.concatenate([jnp.where(lane < B_QK_DIM, q, zero), jnp.where(lane >= B_QK_DIM, q, zero)], axis=0)


def _attn_finish(acc_t, den, lam, g_ref, o_ref):
    tq = acc_t.shape[1] // 2
    o2 = acc_t / den
    o = o2[:, :tq] - lam * o2[:, tq:]
    z = o * lax.rsqrt(jnp.mean(o * o, axis=0, keepdims=True) + EPS)
    o_ref[0] = (z.T * g_ref[...]).astype(BF16)


_NT = (((1,), (1,)), ((), ()))


def _attn_kernel(lam_ref, q_ref, *refs, nseg, shift):
    kv = refs[:2 * nseg]
    g_ref, o_ref = refs[2 * nseg], refs[2 * nseg + 1]
    q2 = _stack_maps(q_ref[0])
    if shift:
        ss = [lax.dot_general(kv[2 * s][0], q2, _NT, preferred_element_type=F32) for s in range(nseg)]
        m = ss[0].max(axis=0, keepdims=True)
        for s in ss[1:]:
            m = jnp.maximum(m, s.max(axis=0, keepdims=True))
        acc = None
        den = None
        for si, s in enumerate(ss):
            e = jnp.exp2(s - m)
            d = e.sum(axis=0, keepdims=True)
            o = jnp.dot(kv[2 * si + 1][0], e.astype(BF16), preferred_element_type=F32)
            acc = o if acc is None else acc + o
            den = d if den is None else den + d
    else:
        acc = None
        part = None
        for si in range(nseg):
            k_ref, vt_ref = kv[2 * si], kv[2 * si + 1]
            lk = k_ref.shape[1]
            kc = min(ATTN_KEY_CHUNK, lk)
            for c0 in range(0, lk, kc):
                s = lax.dot_general(k_ref[0, c0:c0 + kc, :], q2, _NT, preferred_element_type=F32)
                e = jnp.exp2(s)
                p8 = e.reshape(kc // 8, 8, e.shape[1]).sum(axis=0)
                part = p8 if part is None else part + p8
                o = jnp.dot(vt_ref[0, :, c0:c0 + kc], e.astype(BF16), preferred_element_type=F32)
                acc = o if acc is None else acc + o
        den = part.sum(axis=0, keepdims=True)
    _attn_finish(acc, den, lam_ref[0], g_ref, o_ref)


def _attn(lam, q, segs, g_sub, tq, shift):
    nb, n, bw = q.shape
    in_specs = [
        pl.BlockSpec(memory_space=pltpu.SMEM),
        pl.BlockSpec((1, tq, B_V_DIM), lambda b, h, i: (b, i, h)),
    ]
    args = [lam, q]
    for (k, vt) in segs:
        lk = k.shape[1]
        in_specs += [pl.BlockSpec((1, lk, B_V_DIM), lambda b, h, i: (b, 0, h)),
                     pl.BlockSpec((1, B_V_DIM, lk), lambda b, h, i: (b, h, 0))]
        args += [k, vt]
    in_specs.append(pl.BlockSpec((1, B_V_DIM), lambda b, h, i: (0, 0)))
    args.append(g_sub)
    return pl.pallas_call(
        functools.partial(_attn_kernel, nseg=len(segs), shift=shift),
        grid=(nb, B_HEADS, n // tq),
        in_specs=in_specs,
        out_specs=pl.BlockSpec((1, tq, B_V_DIM), lambda b, h, i: (b, i, h)),
        out_shape=jax.ShapeDtypeStruct((nb, n, bw), BF16),
        compiler_params=_cparams(("parallel", "parallel", "parallel")),
        name="attn%d%s" % (len(segs), "_shift" if shift else ""),
    )(*args)


def _attn_guarded(score_bound, lam, q, segs, g_sub, tq, tq_shift):
    flat = [a for kv in segs for a in kv]

    def run(shift, lam, q, g_sub, *flat):
        segs = [(flat[2 * s], flat[2 * s + 1]) for s in range(len(flat) // 2)]
        return _attn(lam, q, segs, g_sub, tq_shift if shift else tq, shift)

    return lax.cond(score_bound <= ATTN_UNSHIFTED_MAX_SCORE,
                    functools.partial(run, False), functools.partial(run, True), lam, q, g_sub, *flat)


def _merge_kernel(x_ref, mod_ref, ya_ref, yb_ref, glu_ref, glup_ref, glun_ref, cw_ref, cb_ref,
                  lcg_ref, lcb_ref, wo_ref, o_ref, ext_ref, sh_ref, *, D):
    i = pl.program_id(1)
    nt = pl.num_programs(1)
    tm = x_ref.shape[1]
    prev = glup_ref[0].astype(F32)
    nxt = glun_ref[0].astype(F32)
    ext_ref[0:HALO, :] = jnp.where(i > 0, prev, jnp.zeros_like(prev))
    ext_ref[HALO:HALO + tm, :] = glu_ref[0].astype(F32)
    ext_ref[HALO + tm:, :] = jnp.where(i < nt - 1, nxt, jnp.zeros_like(nxt))
    base = HALO - C_CONV // 2
    srcs = {}
    for r in range(8):
        off = base + r
        n_taps = len(range(r, C_CONV, 8))
        if off % 8 == 0:
            srcs[r] = (ext_ref, off)
        else:
            span = tm + 8 * (n_taps - 1)
            sh_ref[r, 0:span, :] = ext_ref[off:off + span, :]
            srcs[r] = (sh_ref.at[r], 0)
    acc = jnp.broadcast_to(cb_ref[...], (tm, cb_ref.shape[1]))
    for j in range(C_CONV):
        src, off = srcs[j % 8]
        r0 = off + 8 * (j // 8)
        acc = acc + cw_ref[j:j + 1, :] * src[r0:r0 + tm, :]
    yc = _silu(_ln(acc, lcg_ref[...], lcb_ref[...]))
    mix = jnp.concatenate([ya_ref[0], yb_ref[0], yc.astype(BF16)], axis=-1)
    out = jnp.dot(mix, wo_ref[...], preferred_element_type=F32)
    gate = mod_ref[0][:, 2 * D:3 * D]
    o_ref[0] = x_ref[0] + gate * out


def _merge(layer, x, mod, ya, yb, glu, cw, cb, lcg, lcb, w_out, tm):
    nb, n, D = x.shape
    aw, bw = D // 4, D // 2
    r = tm // HALO
    nh = n // HALO
    const = lambda b, i: (0, 0)
    tok = lambda b, i: (b, i, 0)
    return pl.pallas_call(
        functools.partial(_merge_kernel, D=D),
        grid=(nb, n // tm),
        in_specs=[
            pl.BlockSpec((1, tm, D), tok),
            pl.BlockSpec((1, 1, 6 * D), lambda b, i: (b, 0, 0)),
            pl.BlockSpec((1, tm, aw), tok),
            pl.BlockSpec((1, tm, bw), tok),
            pl.BlockSpec((1, tm, aw), tok),
            pl.BlockSpec((1, HALO, aw), lambda b, i: (b, jnp.maximum(i * r - 1, 0), 0)),
            pl.BlockSpec((1, HALO, aw), lambda b, i: (b, jnp.minimum((i + 1) * r, nh - 1), 0)),
            pl.BlockSpec((C_CONV, aw), const),
            pl.BlockSpec((1, aw), const),
            pl.BlockSpec((1, aw), const),
            pl.BlockSpec((1, aw), const),
            pl.BlockSpec((None, D, D), lambda b, i: (layer, 0, 0)),
        ],
        out_specs=pl.BlockSpec((1, tm, D), tok),
        out_shape=jax.ShapeDtypeStruct((nb, n, D), F32),
        scratch_shapes=[pltpu.VMEM((tm + 2 * HALO, aw), F32),
                        pltpu.VMEM((8, tm + 8 * (C_CONV // 8), aw), F32)],
        compiler_params=_cparams(("parallel", "parallel")),
        name="merge",
    )(x, mod, ya, yb, glu, glu, glu, cw, cb, lcg, lcb, w_out)


def _ffn_kernel(x_ref, xp_ref, xn_ref, mod_ref, g2_ref, wg_ref, wv_ref, wd_ref, cw_ref, cb_ref,
                o_ref, gt_ref, *, D):
    i = pl.program_id(1)
    nt = pl.num_programs(1)
    tm = x_ref.shape[1]
    mod = mod_ref[0]
    shift, scale, gate = mod[:, 3 * D:4 * D], mod[:, 4 * D:5 * D], mod[:, 5 * D:6 * D]

    def h2(xr):
        return (_rms(xr, g2_ref[...]) * (1.0 + scale) + shift).astype(BF16)

    x = x_ref[0]
    hp = h2(xp_ref[0])
    hn = h2(xn_ref[0])
    hp = jnp.where(i > 0, hp, jnp.zeros_like(hp))
    hn = jnp.where(i < nt - 1, hn, jnp.zeros_like(hn))
    hm = h2(x)
    hext = jnp.concatenate([hp, hm, hn], axis=0)
    gt_ref[...] = jnp.dot(hext, wg_ref[...], preferred_element_type=F32)
    val = jnp.dot(hm, wv_ref[...], preferred_element_type=F32)
    cw = cw_ref[...]
    conv = (cw[0:1] * gt_ref[HALO - 1:HALO - 1 + tm, :] + cw[1:2] * gt_ref[HALO:HALO + tm, :]
            + cw[2:3] * gt_ref[HALO + 1:HALO + 1 + tm, :] + cb_ref[...])
    act = (_silu(conv) * val).astype(BF16)
    o_ref[0] = x + gate * jnp.dot(act, wd_ref[...], preferred_element_type=F32)


def _ffn(layer, x, mod, g2, wg, wv, wd, cw, cb, tm):
    nb, n, D = x.shape
    F = wg.shape[2]
    of_layer = lambda b, i: (layer, 0, 0)
    r = tm // HALO
    nh = n // HALO
    const = lambda b, i: (0, 0)
    tok = lambda b, i: (b, i, 0)
    resident = dict(pipeline_mode=pl.Buffered(1))
    return pl.pallas_call(
        functools.partial(_ffn_kernel, D=D),
        grid=(nb, n // tm),
        in_specs=[
            pl.BlockSpec((1, tm, D), tok),
            pl.BlockSpec((1, HALO, D), lambda b, i: (b, jnp.maximum(i * r - 1, 0), 0)),
            pl.BlockSpec((1, HALO, D), lambda b, i: (b, jnp.minimum((i + 1) * r, nh - 1), 0)),
            pl.BlockSpec((1, 1, 6 * D), lambda b, i: (b, 0, 0)),
            pl.BlockSpec((1, D), const),
            pl.BlockSpec((None, D, F), of_layer, **resident),
            pl.BlockSpec((None, D, F), of_layer, **resident),
            pl.BlockSpec((None, F, D), of_layer, **resident),
            pl.BlockSpec((FFN_CONV, F), const),
            pl.BlockSpec((1, F), const),
        ],
        out_specs=pl.BlockSpec((1, tm, D), tok),
        out_shape=jax.ShapeDtypeStruct((nb, n, D), F32),
        scratch_shapes=[pltpu.VMEM((tm + 2 * HALO, F), F32)],
        compiler_params=_cparams(("parallel", "parallel")),
        name="ffn",
    )(x, x, x, mod, g2, wg, wv, wd, cw, cb)


def _rope_tables(n):
    t = jnp.arange(n)
    pos = jnp.stack([(t // GRID_W).astype(F32), (t % GRID_W).astype(F32)], axis=1)
    half_dim = B_QK_DIM // 2
    inv = ROPE_THETA ** (-jnp.arange(0, half_dim, 2, dtype=F32) / half_dim)
    lane = jnp.arange(128)
    g64 = lane % B_QK_DIM
    axis = g64 // half_dim
    second = (g64 % half_dim) // (half_dim // 2)
    ang = pos[:, axis] * inv[g64 % (half_dim // 2)][None, :]
    cos, sin = jnp.cos(ang), jnp.sin(ang)
    zero = jnp.zeros_like(sin)
    return cos, jnp.where(second[None, :] == 0, -sin, zero), jnp.where(second[None, :] == 1, sin, zero)


def kernel(x, c, ctx, c_ctx, w_mod, b_mod, norm1_g, w_in, ln_v_g, ln_v_b, w_s, b_s, q_norm_g, k_norm_g,
           lam_q1, lam_k1, lam_q2, lam_k2, subln_g, conv_w, conv_b, ln_c_g, ln_c_b, w_out, norm2_g,
           w_gate, w_val, ffn_conv_w, ffn_conv_b, w_down):
    nb, n, D = x.shape
    n_ctx = ctx.shape[1]
    L = w_mod.shape[0]
    aw = D // 4
    assert nb + 1 <= MOD_ROWS and n % GRID_W == 0

    cvec = jnp.zeros((MOD_ROWS, D), F32).at[:nb].set(c).at[nb].set(c_ctx)
    mods, lam_dyn = _adaln(cvec, w_mod, b_mod, lam_q1, lam_k1, lam_q2, lam_k2)

    tm = min(1024, n)
    tm_c = min(1024, n_ctx)
    tq = min(2048, n)
    tq_c = min(256, n_ctx)
    rope_tabs = _rope_tables(n)
    gsum = jnp.kron(jnp.eye(V7X_MXU_DIM // B_QK_DIM, dtype=F32), jnp.ones((B_QK_DIM, B_QK_DIM), F32)).astype(BF16)
    reps = (D // 2) // B_QK_DIM

    w_in_b, w_out_b, w_gate_b, w_val_b, w_down_b = (_to_bf16(w) for w in (w_in, w_out, w_gate, w_val, w_down))
    ws_b = _to_bf16(jnp.transpose(w_s, (0, 2, 1, 3)).reshape(L, CHUNK, A_HEADS * CHUNK))

    x_lat, x_ctx = x, ctx
    for l in range(L):
        last = l == L - 1
        lam_init = 0.8 - 0.6 * math.exp(-0.3 * l)
        lam = lam_dyn[l, 0, 0:1] + lam_init
        mod_l = mods[l, :nb].reshape(nb, 1, 6 * D)
        mod_c = jnp.broadcast_to(mods[l, nb].reshape(1, 1, 6 * D), (nb, 1, 6 * D))
        row = lambda a: a[l].reshape(1, -1)
        bs_full = jnp.repeat(b_s[l].T, aw // A_HEADS, axis=1)
        gq = jnp.tile(q_norm_g[l], reps).reshape(1, -1) * (B_QK_DIM ** -0.5 * math.log2(math.e))
        score_bound = 1.02 * B_QK_DIM ** 0.5 * jnp.max(jnp.abs(q_norm_g[l])) * jnp.max(jnp.abs(k_norm_g[l]))
        gk = jnp.tile(k_norm_g[l], reps).reshape(1, -1)
        g_sub = row(subln_g) * (1.0 - lam_init)
        proj_args = (row(norm1_g), w_in_b, row(ln_v_g), row(ln_v_b), ws_b, bs_full, gq, gk, gsum)
        merge_args = (conv_w[l], row(conv_b), row(ln_c_g), row(ln_c_b), w_out_b)
        ffn_args = (row(norm2_g), w_gate_b, w_val_b, w_down_b, ffn_conv_w[l], row(ffn_conv_b))

        if last:
            k_c, vt_c = _proj(l, x_ctx, mod_c, *proj_args, None, tm_c, kv_only=True)
        else:
            ya_c, q_c, k_c, vt_c, glu_c = _proj(l, x_ctx, mod_c, *proj_args, None, tm_c)
            yb_c = _attn(lam, q_c, [(k_c, vt_c)], g_sub, tq_c, True)
            x_ctx = _merge(l, x_ctx, mod_c, ya_c, yb_c, glu_c, *merge_args, tm_c)
            x_ctx = _ffn(l, x_ctx, mod_c, *ffn_args, tm_c)

        ya, q, k, vt, glu = _proj(l, x_lat, mod_l, *proj_args, rope_tabs, tm)
        yb = _attn_guarded(score_bound, lam, q, [(k, vt), (k_c, vt_c)], g_sub, tq, tq_c)
        x_lat = _merge(l, x_lat, mod_l, ya, yb, glu, *merge_args, tm)
        x_lat = _ffn(l, x_lat, mod_l, *ffn_args, tm)
    return x_lat
```

```python
import functools
import math

import jax
import jax.numpy as jnp
from jax import lax
from jax.experimental import pallas as pl
from jax.experimental.pallas import tpu as pltpu

F32 = jnp.float32
BF16 = jnp.bfloat16

EPS = 1e-6
GRID_W = 64
ROPE_THETA = 10000.0
A_HEADS = 4
CHUNK = 128
B_HEADS = 4
B_QK_DIM = 64
B_V_DIM = 128
C_CONV = 31
FFN_CONV = 3

V7X_MXU_DIM = 256
V7X_VMEM_LIMIT_BYTES = 56 * 1024 * 1024
BF16_SUBLANES = 16
HALO = 16
MOD_ROWS = 24
ATTN_KEY_CHUNK = 512
CAST_ROW_BLOCK = 256
ATTN_UNSHIFTED_MAX_SCORE = 64.0


def _cparams(sem):
    return pltpu.CompilerParams(dimension_semantics=sem, vmem_limit_bytes=V7X_VMEM_LIMIT_BYTES)


def _rms(x, g):
    return x * lax.rsqrt(jnp.mean(x * x, axis=-1, keepdims=True) + EPS) * g


def _ln(x, g, b):
    xc = x - jnp.mean(x, axis=-1, keepdims=True)
    return xc * lax.rsqrt(jnp.mean(xc * xc, axis=-1, keepdims=True) + EPS) * g + b


def _silu(x):
    return x * jax.nn.sigmoid(x)


def _adaln_kernel(c_ref, w_ref, b_ref, q1_ref, k1_ref, q2_ref, k2_ref, o_ref, lam_ref):
    s = _silu(c_ref[...])
    o_ref[0] = jnp.dot(s.astype(BF16), w_ref[0].astype(BF16), preferred_element_type=F32) + b_ref[0]
    d1 = jnp.sum(q1_ref[0] * k1_ref[0], axis=-1, keepdims=True)
    d2 = jnp.sum(q2_ref[0] * k2_ref[0], axis=-1, keepdims=True)
    lam_ref[0] = jnp.broadcast_to(jnp.exp(d1) - jnp.exp(d2), lam_ref.shape[1:])


def _adaln(cvec, w_mod, b_mod, lq1, lk1, lq2, lk2):
    L, D, N6 = w_mod.shape
    tn = N6 // 4
    small = pl.BlockSpec((1, 1, B_QK_DIM), lambda l, j: (l, 0, 0))
    return pl.pallas_call(
        _adaln_kernel,
        grid=(L, N6 // tn),
        in_specs=[
            pl.BlockSpec((MOD_ROWS, D), lambda l, j: (0, 0)),
            pl.BlockSpec((1, D, tn), lambda l, j: (l, 0, j)),
            pl.BlockSpec((1, 1, tn), lambda l, j: (l, 0, j)),
            small, small, small, small,
        ],
        out_specs=[
            pl.BlockSpec((1, MOD_ROWS, tn), lambda l, j: (l, 0, j)),
            pl.BlockSpec((1, 8, 128), lambda l, j: (l, 0, 0)),
        ],
        out_shape=[
            jax.ShapeDtypeStruct((L, MOD_ROWS, N6), F32),
            jax.ShapeDtypeStruct((L, 8, 128), F32),
        ],
        compiler_params=_cparams(("arbitrary", "arbitrary")),
        name="adaln",
    )(cvec, w_mod, b_mod.reshape(L, 1, N6), lq1.reshape(L, 1, -1), lk1.reshape(L, 1, -1),
      lq2.reshape(L, 1, -1), lk2.reshape(L, 1, -1))


def _cast_kernel(w_ref, o_ref):
    o_ref[...] = w_ref[...].astype(BF16)


def _to_bf16(w):
    L, R, C = w.shape
    tr = min(CAST_ROW_BLOCK, R)
    spec = pl.BlockSpec((1, tr, C), lambda l, i: (l, i, 0))
    return pl.pallas_call(
        _cast_kernel,
        grid=(L, R // tr),
        in_specs=[spec],
        out_specs=spec,
        out_shape=jax.ShapeDtypeStruct(w.shape, BF16),
        compiler_params=_cparams(("parallel", "parallel")),
        name="cast_bf16",
    )(w)


def _proj_kernel(*refs, rope, kv_only, D):
    (x_ref, mod_ref, g1_ref, w_ref, lng_ref, lnb_ref, ws_ref, bs_ref, gq_ref, gk_ref, gsum_ref) = refs[:11]
    if rope:
        cos_ref, sa_ref, sb_ref = refs[11:14]
        outs = refs[14:]
    else:
        outs = refs[11:]
    if kv_only:
        k_ref, vt_ref = outs
    else:
        ya_ref, q_ref, k_ref, vt_ref, glu_ref = outs

    aw = D // 4
    bw = D // 2
    x = x_ref[0]
    tm = x.shape[0]
    mod = mod_ref[0]
    shift, scale = mod[:, 0:D], mod[:, D:2 * D]
    h = (_rms(x, g1_ref[...]) * (1.0 + scale) + shift).astype(BF16)
    c_q = 2 * aw
    c_k = c_q + bw
    c_v = c_k + bw
    c_c = c_v + bw

    def cols(c0, c1):
        return jnp.dot(h, w_ref[:, c0:c1], preferred_element_type=F32)

    def epilogue_a(pa):
        z = jax.nn.gelu(pa)
        u, v = z[:, :aw], z[:, aw:]
        vn = _ln(v, lng_ref[...], lnb_ref[...]).astype(BF16)
        head = lax.broadcasted_iota(jnp.int32, (CHUNK, aw), 1) // (aw // A_HEADS)
        ws = ws_ref[...]
        zero = jnp.zeros((CHUNK, aw), BF16)
        for c in range(tm // CHUNK):
            vc = vn[c * CHUNK:(c + 1) * CHUNK, :]
            vbd = jnp.concatenate([jnp.where(head == hh, vc, zero) for hh in range(A_HEADS)], axis=0)
            gate = jnp.dot(ws, vbd, preferred_element_type=F32) + bs_ref[...]
            ya_ref[0, c * CHUNK:(c + 1) * CHUNK, :] = (u[c * CHUNK:(c + 1) * CHUNK] * gate).astype(BF16)

    def qk_norm(t, g):
        sq = (t * t).astype(BF16)
        parts = [jnp.dot(sq[:, j:j + V7X_MXU_DIM], gsum_ref[...], preferred_element_type=F32)
                 for j in range(0, bw, V7X_MXU_DIM)]
        ss = jnp.concatenate(parts, axis=-1)
        t = t * lax.rsqrt(ss * (1.0 / B_QK_DIM) + EPS) * g
        if not rope:
            return t
        cos, sa, sb = cos_ref[...], sa_ref[...], sb_ref[...]
        blocks = []
        for j in range(0, bw, 128):
            tb = t[:, j:j + 128]
            blocks.append(tb * cos + pltpu.roll(tb, 128 - 16, 1) * sa + pltpu.roll(tb, 16, 1) * sb)
        return jnp.concatenate(blocks, axis=-1)

    if kv_only:
        pk = cols(c_k, c_v)
        pv = cols(c_v, c_c)
        k_ref[0] = qk_norm(pk, gk_ref[...]).astype(BF16)
        vt_ref[0] = pv.astype(BF16).T
        return
    pa = cols(0, c_q)
    pq = cols(c_q, c_k)
    epilogue_a(pa)
    pk = cols(c_k, c_v)
    q_ref[0] = qk_norm(pq, gq_ref[...]).astype(BF16)
    pv = cols(c_v, c_c)
    k_ref[0] = qk_norm(pk, gk_ref[...]).astype(BF16)
    pc = cols(c_c, c_c + 2 * aw)
    vt_ref[0] = pv.astype(BF16).T
    glu_ref[0] = (pc[:, :aw] * jax.nn.sigmoid(pc[:, aw:])).astype(BF16)


def _proj(layer, x, mod, g1, w_in, lng, lnb, ws_all, bs_full, gq, gk, gsum, rope_tabs, tm, kv_only=False):
    nb, n, D = x.shape
    aw, bw = D // 4, D // 2
    cols = w_in.shape[2]
    of_layer = lambda b, i: (layer, 0, 0)
    rope = rope_tabs is not None
    const = lambda b, i: (0, 0)
    tok = lambda b, i: (b, i, 0)
    in_specs = [
        pl.BlockSpec((1, tm, D), tok),
        pl.BlockSpec((1, 1, 6 * D), lambda b, i: (b, 0, 0)),
        pl.BlockSpec((1, D), const),
        pl.BlockSpec((None, D, cols), of_layer),
        pl.BlockSpec((1, aw), const),
        pl.BlockSpec((1, aw), const),
        pl.BlockSpec((None, CHUNK, A_HEADS * CHUNK), of_layer),
        pl.BlockSpec((CHUNK, aw), const),
        pl.BlockSpec((1, bw), const),
        pl.BlockSpec((1, bw), const),
        pl.BlockSpec((V7X_MXU_DIM, V7X_MXU_DIM), const),
    ]
    args = [x, mod, g1, w_in, lng, lnb, ws_all, bs_full, gq, gk, gsum]
    if rope:
        in_specs += [pl.BlockSpec((tm, 128), lambda b, i: (i, 0))] * 3
        args += list(rope_tabs)
    tok_spec = lambda w: pl.BlockSpec((1, tm, w), tok)
    tok_shape = lambda w: jax.ShapeDtypeStruct((nb, n, w), BF16)
    out_specs = [tok_spec(aw), tok_spec(bw), tok_spec(bw),
                 pl.BlockSpec((1, bw, tm), lambda b, i: (b, 0, i)), tok_spec(aw)]
    out_shape = [tok_shape(aw), tok_shape(bw), tok_shape(bw),
                 jax.ShapeDtypeStruct((nb, bw, n), BF16), tok_shape(aw)]
    if kv_only:
        out_specs, out_shape = out_specs[2:4], out_shape[2:4]
    return pl.pallas_call(
        functools.partial(_proj_kernel, rope=rope, kv_only=kv_only, D=D),
        grid=(nb, n // tm),
        in_specs=in_specs,
        out_specs=out_specs,
        out_shape=out_shape,
        compiler_params=_cparams(("parallel", "parallel")),
        name=("proj_rope" if rope else "proj") + ("_kv" if kv_only else ""),
    )(*args)


def _stack_maps(q):
    lane = lax.broadcasted_iota(jnp.int32, q.shape, 1)
    zero = jnp.zeros_like(q)
    return jnp.concatenate([jnp.where(lane < B_QK_DIM, q, zero), jnp.where(lane >= B_QK_DIM, q, zero)], axis=0)


def _attn_finish(acc_t, den, lam, g_ref, o_ref):
    tq = acc_t.shape[1] // 2
    o2 = acc_t / den
    o = o2[:, :tq] - lam * o2[:, tq:]
    z = o * lax.rsqrt(jnp.mean(o * o, axis=0, keepdims=True) + EPS)
    o_ref[0] = (z.T * g_ref[...]).astype(BF16)


_NT = (((1,), (1,)), ((), ()))


def _attn_kernel(lam_ref, q_ref, *refs, nseg, shift):
    kv = refs[:2 * nseg]
    g_ref, o_ref = refs[2 * nseg], refs[2 * nseg + 1]
    q2 = _stack_maps(q_ref[0])
    if shift:
        ss = [lax.dot_general(kv[2 * s][0], q2, _NT, preferred_element_type=F32) for s in range(nseg)]
        m = ss[0].max(axis=0, keepdims=True)
        for s in ss[1:]:
            m = jnp.maximum(m, s.max(axis=0, keepdims=True))
        acc = None
        den = None
        for si, s in enumerate(ss):
            e = jnp.exp2(s - m)
            d = e.sum(axis=0, keepdims=True)
            o = jnp.dot(kv[2 * si + 1][0], e.astype(BF16), preferred_element_type=F32)
            acc = o if acc is None else acc + o
            den = d if den is None else den + d
    else:
        acc = None
        part = None
        for si in range(nseg):
            k_ref, vt_ref = kv[2 * si], kv[2 * si + 1]
            lk = k_ref.shape[1]
            kc = min(ATTN_KEY_CHUNK, lk)
            for c0 in range(0, lk, kc):
                s = lax.dot_general(k_ref[0, c0:c0 + kc, :], q2, _NT, preferred_element_type=F32)
                e = jnp.exp2(s)
                p8 = e.reshape(kc // 8, 8, e.shape[1]).sum(axis=0)
                part = p8 if part is None else part + p8
                o = jnp.dot(vt_ref[0, :, c0:c0 + kc], e.astype(BF16), preferred_element_type=F32)
                acc = o if acc is None else acc + o
        den = part.sum(axis=0, keepdims=True)
    _attn_finish(acc, den, lam_ref[0], g_ref, o_ref)


def _attn(lam, q, segs, g_sub, tq, shift):
    nb, n, bw = q.shape
    in_specs = [
        pl.BlockSpec(memory_space=pltpu.SMEM),
        pl.BlockSpec((1, tq, B_V_DIM), lambda b, h, i: (b, i, h)),
    ]
    args = [lam, q]
    for (k, vt) in segs:
        lk = k.shape[1]
        in_specs += [pl.BlockSpec((1, lk, B_V_DIM), lambda b, h, i: (b, 0, h)),
                     pl.BlockSpec((1, B_V_DIM, lk), lambda b, h, i: (b, h, 0))]
        args += [k, vt]
    in_specs.append(pl.BlockSpec((1, B_V_DIM), lambda b, h, i: (0, 0)))
    args.append(g_sub)
    return pl.pallas_call(
        functools.partial(_attn_kernel, nseg=len(segs), shift=shift),
        grid=(nb, B_HEADS, n // tq),
        in_specs=in_specs,
        out_specs=pl.BlockSpec((1, tq, B_V_DIM), lambda b, h, i: (b, i, h)),
        out_shape=jax.ShapeDtypeStruct((nb, n, bw), BF16),
        compiler_params=_cparams(("parallel", "parallel", "parallel")),
        name="attn%d%s" % (len(segs), "_shift" if shift else ""),
    )(*args)


def _attn_guarded(score_bound, lam, q, segs, g_sub, tq, tq_shift):
    flat = [a for kv in segs for a in kv]

    def run(shift, lam, q, g_sub, *flat):
        segs = [(flat[2 * s], flat[2 * s + 1]) for s in range(len(flat) // 2)]
        return _attn(lam, q, segs, g_sub, tq_shift if shift else tq, shift)

    return lax.cond(score_bound <= ATTN_UNSHIFTED_MAX_SCORE,
                    functools.partial(run, False), functools.partial(run, True), lam, q, g_sub, *flat)


def _merge_kernel(x_ref, mod_ref, ya_ref, yb_ref, glu_ref, glup_ref, glun_ref, cw_ref, cb_ref,
                  lcg_ref, lcb_ref, wo_ref, o_ref, ext_ref, sh_ref, *, D):
    i = pl.program_id(1)
    nt = pl.num_programs(1)
    tm = x_ref.shape[1]
    prev = glup_ref[0].astype(F32)
    nxt = glun_ref[0].astype(F32)
    ext_ref[0:HALO, :] = jnp.where(i > 0, prev, jnp.zeros_like(prev))
    ext_ref[HALO:HALO + tm, :] = glu_ref[0].astype(F32)
    ext_ref[HALO + tm:, :] = jnp.where(i < nt - 1, nxt, jnp.zeros_like(nxt))
    base = HALO - C_CONV // 2
    srcs = {}
    for r in range(8):
        off = base + r
        n_taps = len(range(r, C_CONV, 8))
        if off % 8 == 0:
            srcs[r] = (ext_ref, off)
        else:
            span = tm + 8 * (n_taps - 1)
            sh_ref[r, 0:span, :] = ext_ref[off:off + span, :]
            srcs[r] = (sh_ref.at[r], 0)
    acc = jnp.broadcast_to(cb_ref[...], (tm, cb_ref.shape[1]))
    for j in range(C_CONV):
        src, off = srcs[j % 8]
        r0 = off + 8 * (j // 8)
        acc = acc + cw_ref[j:j + 1, :] * src[r0:r0 + tm, :]
    yc = _silu(_ln(acc, lcg_ref[...], lcb_ref[...]))
    mix = jnp.concatenate([ya_ref[0], yb_ref[0], yc.astype(BF16)], axis=-1)
    out = jnp.dot(mix, wo_ref[...], preferred_element_type=F32)
    gate = mod_ref[0][:, 2 * D:3 * D]
    o_ref[0] = x_ref[0] + gate * out


def _merge(layer, x, mod, ya, yb, glu, cw, cb, lcg, lcb, w_out, tm):
    nb, n, D = x.shape
    aw, bw = D // 4, D // 2
    r = tm // HALO
    nh = n // HALO
    const = lambda b, i: (0, 0)
    tok = lambda b, i: (b, i, 0)
    return pl.pallas_call(
        functools.partial(_merge_kernel, D=D),
        grid=(nb, n // tm),
        in_specs=[
            pl.BlockSpec((1, tm, D), tok),
            pl.BlockSpec((1, 1, 6 * D), lambda b, i: (b, 0, 0)),
            pl.BlockSpec((1, tm, aw), tok),
            pl.BlockSpec((1, tm, bw), tok),
            pl.BlockSpec((1, tm, aw), tok),
            pl.BlockSpec((1, HALO, aw), lambda b, i: (b, jnp.maximum(i * r - 1, 0), 0)),
            pl.BlockSpec((1, HALO, aw), lambda b, i: (b, jnp.minimum((i + 1) * r, nh - 1), 0)),
            pl.BlockSpec((C_CONV, aw), const),
            pl.BlockSpec((1, aw), const),
            pl.BlockSpec((1, aw), const),
            pl.BlockSpec((1, aw), const),
            pl.BlockSpec((None, D, D), lambda b, i: (layer, 0, 0)),
        ],
        out_specs=pl.BlockSpec((1, tm, D), tok),
        out_shape=jax.ShapeDtypeStruct((nb, n, D), F32),
        scratch_shapes=[pltpu.VMEM((tm + 2 * HALO, aw), F32),
                        pltpu.VMEM((8, tm + 8 * (C_CONV // 8), aw), F32)],
        compiler_params=_cparams(("parallel", "parallel")),
        name="merge",
    )(x, mod, ya, yb, glu, glu, glu, cw, cb, lcg, lcb, w_out)


def _ffn_kernel(x_ref, xp_ref, xn_ref, mod_ref, g2_ref, wg_ref, wv_ref, wd_ref, cw_ref, cb_ref,
                o_ref, gt_ref, *, D):
    i = pl.program_id(1)
    nt = pl.num_programs(1)
    tm = x_ref.shape[1]
    mod = mod_ref[0]
    shift, scale, gate = mod[:, 3 * D:4 * D], mod[:, 4 * D:5 * D], mod[:, 5 * D:6 * D]

    def h2(xr):
        return (_rms(xr, g2_ref[...]) * (1.0 + scale) + shift).astype(BF16)

    x = x_ref[0]
    hp = h2(xp_ref[0])
    hn = h2(xn_ref[0])
    hp = jnp.where(i > 0, hp, jnp.zeros_like(hp))
    hn = jnp.where(i < nt - 1, hn, jnp.zeros_like(hn))
    hm = h2(x)
    hext = jnp.concatenate([hp, hm, hn], axis=0)
    gt_ref[...] = jnp.dot(hext, wg_ref[...], preferred_element_type=F32)
    val = jnp.dot(hm, wv_ref[...], preferred_element_type=F32)
    cw = cw_ref[...]
    conv = (cw[0:1] * gt_ref[HALO - 1:HALO - 1 + tm, :] + cw[1:2] * gt_ref[HALO:HALO + tm, :]
            + cw[2:3] * gt_ref[HALO + 1:HALO + 1 + tm, :] + cb_ref[...])
    act = (_silu(conv) * val).astype(BF16)
    o_ref[0] = x + gate * jnp.dot(act, wd_ref[...], preferred_element_type=F32)


def _ffn(layer, x, mod, g2, wg, wv, wd, cw, cb, tm):
    nb, n, D = x.shape
    F = wg.shape[2]
    of_layer = lambda b, i: (layer, 0, 0)
    r = tm // HALO
    nh = n // HALO
    const = lambda b, i: (0, 0)
    tok = lambda b, i: (b, i, 0)
    resident = dict(pipeline_mode=pl.Buffered(1))
    return pl.pallas_call(
        functools.partial(_ffn_kernel, D=D),
        grid=(nb, n // tm),
        in_specs=[
            pl.BlockSpec((1, tm, D), tok),
            pl.BlockSpec((1, HALO, D), lambda b, i: (b, jnp.maximum(i * r - 1, 0), 0)),
            pl.BlockSpec((1, HALO, D), lambda b, i: (b, jnp.minimum((i + 1) * r, nh - 1), 0)),
            pl.BlockSpec((1, 1, 6 * D), lambda b, i: (b, 0, 0)),
            pl.BlockSpec((1, D), const),
            pl.BlockSpec((None, D, F), of_layer, **resident),
            pl.BlockSpec((None, D, F), of_layer, **resident),
            pl.BlockSpec((None, F, D), of_layer, **resident),
            pl.BlockSpec((FFN_CONV, F), const),
            pl.BlockSpec((1, F), const),
        ],
        out_specs=pl.BlockSpec((1, tm, D), tok),
        out_shape=jax.ShapeDtypeStruct((nb, n, D), F32),
        scratch_shapes=[pltpu.VMEM((tm + 2 * HALO, F), F32)],
        compiler_params=_cparams(("parallel", "parallel")),
        name="ffn",
    )(x, x, x, mod, g2, wg, wv, wd, cw, cb)


def _mergeffn_kernel(x_ref, xp_ref, xn_ref, mod_ref, ya_ref, yap_ref, yan_ref, yb_ref, ybp_ref, ybn_ref,
                     glu_ref, glup_ref, glun_ref, cw_ref, cb_ref, lcg_ref, lcb_ref, wo_ref,
                     g2_ref, wg_ref, wv_ref, wd_ref, fw_ref, fb_ref, o_ref, ext_ref, sh_ref, gt_ref, *, D):
    i = pl.program_id(1)
    nt = pl.num_programs(1)
    tm = x_ref.shape[1]
    R = tm + 2 * HALO
    first, last = i == 0, i == nt - 1
    mod = mod_ref[0]
    g1 = mod[:, 2 * D:3 * D]
    shift, scale, g2 = mod[:, 3 * D:4 * D], mod[:, 4 * D:5 * D], mod[:, 5 * D:6 * D]
    gp = glup_ref[0].astype(F32)
    gn = glun_ref[0].astype(F32)
    ext_ref[0:2 * HALO, :] = jnp.where(first, jnp.zeros_like(gp), gp)
    ext_ref[2 * HALO:2 * HALO + tm, :] = glu_ref[0].astype(F32)
    ext_ref[2 * HALO + tm:, :] = jnp.where(last, jnp.zeros_like(gn), gn)
    base = HALO - C_CONV // 2
    srcs = {}
    for r in range(8):
        off = base + r
        n_taps = len(range(r, C_CONV, 8))
        if off % 8 == 0:
            srcs[r] = (ext_ref, off)
        else:
            span = R + 8 * (n_taps - 1)
            sh_ref[r, 0:span, :] = ext_ref[off:off + span, :]
            srcs[r] = (sh_ref.at[r], 0)
    acc = jnp.broadcast_to(cb_ref[...], (R, cb_ref.shape[1]))
    for j in range(C_CONV):
        src, off = srcs[j % 8]
        r0 = off + 8 * (j // 8)
        acc = acc + cw_ref[j:j + 1, :] * src[r0:r0 + R, :]
    yc = _silu(_ln(acc, lcg_ref[...], lcb_ref[...])).astype(BF16)
    ya = jnp.concatenate([yap_ref[0], ya_ref[0], yan_ref[0]], axis=0)
    yb = jnp.concatenate([ybp_ref[0], yb_ref[0], ybn_ref[0]], axis=0)
    xe = jnp.concatenate([xp_ref[0], x_ref[0], xn_ref[0]], axis=0)
    x1 = xe + g1 * jnp.dot(jnp.concatenate([ya, yb, yc], axis=-1), wo_ref[...], preferred_element_type=F32)
    h = (_rms(x1, g2_ref[...]) * (1.0 + scale) + shift).astype(BF16)
    row = lax.broadcasted_iota(jnp.int32, (R, 1), 0)
    outside = (first & (row < HALO)) | (last & (row >= HALO + tm))
    h = jnp.where(outside, jnp.zeros_like(h), h)
    gt_ref[...] = jnp.dot(h, wg_ref[...], preferred_element_type=F32)
    val = jnp.dot(h[HALO:HALO + tm], wv_ref[...], preferred_element_type=F32)
    fw = fw_ref[...]
    conv = (fw[0:1] * gt_ref[HALO - 1:HALO - 1 + tm, :] + fw[1:2] * gt_ref[HALO:HALO + tm, :]
            + fw[2:3] * gt_ref[HALO + 1:HALO + 1 + tm, :] + fb_ref[...])
    act = (_silu(conv) * val).astype(BF16)
    o_ref[0] = x1[HALO:HALO + tm] + g2 * jnp.dot(act, wd_ref[...], preferred_element_type=F32)


def _mergeffn(layer, x, mod, ya, yb, glu, cw, cb, lcg, lcb, w_out, g2, wg, wv, wd, fw, fb, tm):
    nb, n, D = x.shape
    aw, bw = D // 4, D // 2
    F = wg.shape[2]
    of_layer = lambda b, i: (layer, 0, 0)
    const = lambda b, i: (0, 0)
    tok = lambda b, i: (b, i, 0)
    resident = dict(pipeline_mode=pl.Buffered(1))

    def halo3(w, rows):
        r, nh = tm // rows, n // rows
        return [pl.BlockSpec((1, tm, w), tok),
                pl.BlockSpec((1, rows, w), lambda b, i: (b, jnp.maximum(i * r - 1, 0), 0)),
                pl.BlockSpec((1, rows, w), lambda b, i: (b, jnp.minimum((i + 1) * r, nh - 1), 0))]

    xs = halo3(D, HALO)
    in_specs = (xs + [pl.BlockSpec((1, 1, 6 * D), lambda b, i: (b, 0, 0))] + halo3(aw, HALO) + halo3(bw, HALO)
                + halo3(aw, 2 * HALO)
                + [pl.BlockSpec((C_CONV, aw), const), pl.BlockSpec((1, aw), const), pl.BlockSpec((1, aw), const),
                   pl.BlockSpec((1, aw), const), pl.BlockSpec((None, D, D), of_layer, **resident),
                   pl.BlockSpec((1, D), const),
                   pl.BlockSpec((None, D, F), of_layer, **resident), pl.BlockSpec((None, D, F), of_layer, **resident),
                   pl.BlockSpec((None, F, D), of_layer, **resident),
                   pl.BlockSpec((FFN_CONV, F), const), pl.BlockSpec((1, F), const)])
    R = tm + 2 * HALO
    return pl.pallas_call(
        functools.partial(_mergeffn_kernel, D=D),
        grid=(nb, n // tm),
        in_specs=in_specs,
        out_specs=pl.BlockSpec((1, tm, D), tok),
        out_shape=jax.ShapeDtypeStruct((nb, n, D), F32),
        scratch_shapes=[pltpu.VMEM((tm + 4 * HALO, aw), F32),
                        pltpu.VMEM((8, R + 8 * (C_CONV // 8), aw), F32),
                        pltpu.VMEM((R, F), F32)],
        compiler_params=_cparams(("parallel", "parallel")),
        name="mergeffn",
    )(x, x, x, mod, ya, ya, ya, yb, yb, yb, glu, glu, glu, cw, cb, lcg, lcb, w_out, g2, wg, wv, wd, fw, fb)


def _rope_tables(n):
    t = jnp.arange(n)
    pos = jnp.stack([(t // GRID_W).astype(F32), (t % GRID_W).astype(F32)], axis=1)
    half_dim = B_QK_DIM // 2
    inv = ROPE_THETA ** (-jnp.arange(0, half_dim, 2, dtype=F32) / half_dim)
    lane = jnp.arange(128)
    g64 = lane % B_QK_DIM
    axis = g64 // half_dim
    second = (g64 % half_dim) // (half_dim // 2)
    ang = pos[:, axis] * inv[g64 % (half_dim // 2)][None, :]
    cos, sin = jnp.cos(ang), jnp.sin(ang)
    zero = jnp.zeros_like(sin)
    return cos, jnp.where(second[None, :] == 0, -sin, zero), jnp.where(second[None, :] == 1, sin, zero)


def kernel(x, c, ctx, c_ctx, w_mod, b_mod, norm1_g, w_in, ln_v_g, ln_v_b, w_s, b_s, q_norm_g, k_norm_g,
           lam_q1, lam_k1, lam_q2, lam_k2, subln_g, conv_w, conv_b, ln_c_g, ln_c_b, w_out, norm2_g,
           w_gate, w_val, ffn_conv_w, ffn_conv_b, w_down):
    nb, n, D = x.shape
    n_ctx = ctx.shape[1]
    L = w_mod.shape[0]
    aw = D // 4
    assert nb + 1 <= MOD_ROWS and n % GRID_W == 0

    cvec = jnp.zeros((MOD_ROWS, D), F32).at[:nb].set(c).at[nb].set(c_ctx)
    mods, lam_dyn = _adaln(cvec, w_mod, b_mod, lam_q1, lam_k1, lam_q2, lam_k2)

    tm = min(1024, n)
    tm_c = min(1024, n_ctx)
    tm_f = min(512, n)
    tq = min(2048, n)
    tq_c = min(256, n_ctx)
    rope_tabs = _rope_tables(n)
    gsum = jnp.kron(jnp.eye(V7X_MXU_DIM // B_QK_DIM, dtype=F32), jnp.ones((B_QK_DIM, B_QK_DIM), F32)).astype(BF16)
    reps = (D // 2) // B_QK_DIM

    w_in_b, w_out_b, w_gate_b, w_val_b, w_down_b = (_to_bf16(w) for w in (w_in, w_out, w_gate, w_val, w_down))
    ws_b = _to_bf16(jnp.transpose(w_s, (0, 2, 1, 3)).reshape(L, CHUNK, A_HEADS * CHUNK))

    x_lat, x_ctx = x, ctx
    for l in range(L):
        last = l == L - 1
        lam_init = 0.8 - 0.6 * math.exp(-0.3 * l)
        lam = lam_dyn[l, 0, 0:1] + lam_init
        mod_l = mods[l, :nb].reshape(nb, 1, 6 * D)
        mod_c = jnp.broadcast_to(mods[l, nb].reshape(1, 1, 6 * D), (nb, 1, 6 * D))
        row = lambda a: a[l].reshape(1, -1)
        bs_full = jnp.repeat(b_s[l].T, aw // A_HEADS, axis=1)
        gq = jnp.tile(q_norm_g[l], reps).reshape(1, -1) * (B_QK_DIM ** -0.5 * math.log2(math.e))
        score_bound = 1.02 * B_QK_DIM ** 0.5 * jnp.max(jnp.abs(q_norm_g[l])) * jnp.max(jnp.abs(k_norm_g[l]))
        gk = jnp.tile(k_norm_g[l], reps).reshape(1, -1)
        g_sub = row(subln_g) * (1.0 - lam_init)
        proj_args = (row(norm1_g), w_in_b, row(ln_v_g), row(ln_v_b), ws_b, bs_full, gq, gk, gsum)
        merge_args = (conv_w[l], row(conv_b), row(ln_c_g), row(ln_c_b), w_out_b)
        ffn_args = (row(norm2_g), w_gate_b, w_val_b, w_down_b, ffn_conv_w[l], row(ffn_conv_b))

        if last:
            k_c, vt_c = _proj(l, x_ctx, mod_c, *proj_args, None, tm_c, kv_only=True)
        else:
            ya_c, q_c, k_c, vt_c, glu_c = _proj(l, x_ctx, mod_c, *proj_args, None, tm_c)
            yb_c = _attn(lam, q_c, [(k_c, vt_c)], g_sub, tq_c, True)
            x_ctx = _mergeffn(l, x_ctx, mod_c, ya_c, yb_c, glu_c, *merge_args, *ffn_args, tm_c)

        ya, q, k, vt, glu = _proj(l, x_lat, mod_l, *proj_args, rope_tabs, tm)
        yb = _attn_guarded(score_bound, lam, q, [(k, vt), (k_c, vt_c)], g_sub, tq, tq_c)
        x_lat = _mergeffn(l, x_lat, mod_l, ya, yb, glu, *merge_args, *ffn_args, tm_f)
    return x_lat
```
